```python
import jax
import jax.numpy as jnp
from jax import lax

D_MODEL = 1024
BATCH = 2
SEQ = 16384
DEPTH = 2

GRID_W = 64
CTX_LEN = 256
EPS = 1e-6

CONV_DIM = 512
CONV_WIDTH = 3

HGRN_DIM = 512
HGRN_EXPAND = 128
HGRN_HEADS = HGRN_DIM // HGRN_EXPAND
HGRN_CHUNK = 64

ATT_HEADS = 8
ATT_KV_HEADS = 2
ATT_GROUP = ATT_HEADS // ATT_KV_HEADS
HEAD_DIM = 64
ATT_DIM = ATT_HEADS * HEAD_DIM
KV_DIM = ATT_KV_HEADS * HEAD_DIM
Q_BLOCK = 128
ROPE_THETA = 10000.0
ROPE_AXIS_DIM = HEAD_DIM // 2

N_BRANCH = 3
IN_NAMES = ('a_val', 'a_b', 'a_c', 'b_q', 'b_f_fwd', 'b_f_bwd', 'b_i', 'b_g', 'c_q', 'c_k', 'c_v', 'gate_a', 'gate_b', 'gate_c')
IN_SIZES = (CONV_DIM, CONV_DIM, CONV_DIM, HGRN_DIM, HGRN_DIM, HGRN_DIM, HGRN_DIM, HGRN_DIM, ATT_DIM, KV_DIM, KV_DIM, D_MODEL, D_MODEL, D_MODEL)
W_IN_COLS = 3 * CONV_DIM + 5 * HGRN_DIM + ATT_DIM + 2 * KV_DIM + N_BRANCH * D_MODEL

N_EXPERTS = 16
N_GROUPS = 4
EXPERTS_PER_GROUP = N_EXPERTS // N_GROUPS
TOP_K = 2
EXPERT_DFF = 1024
MOE_BLOCK = 128

kernel_name = 'hybrid_conv_hgrn2_gqa_moe_dit'


def rms_norm(x, g):
    xf = x.astype(jnp.float32)
    y = xf * lax.rsqrt(jnp.mean(xf * xf, axis=-1, keepdims=True) + EPS)
    return (y * g.astype(jnp.float32)).astype(x.dtype)


def _project(u, w_in):
    z = u @ w_in
    parts = {}
    off = 0
    for name, size in zip(IN_NAMES, IN_SIZES):
        parts[name] = z[..., off:off + size]
        off += size
    return parts


def _short_conv(u, w):
    L = u.shape[1]
    pad = CONV_WIDTH // 2
    up = jnp.pad(u, ((0, 0), (pad, CONV_WIDTH - 1 - pad), (0, 0)))
    y = up[:, 0:L] * w[0]
    for j in range(1, CONV_WIDTH):
        y = y + up[:, j:j + L] * w[j]
    return y


def _conv_mixer(p, conv_w, w_out):
    return (p['a_b'] * _short_conv(p['a_c'] * p['a_val'], conv_w)) @ w_out


def _hgrn_lower_bounds(lb_param):
    p = jax.nn.softmax(lb_param.astype(jnp.float32), axis=0)
    cum = jnp.cumsum(p, axis=0)
    return cum - cum[0:1]


def _gla_chunk_scan(q, k, v, log_f, s0):
    B, L, H, _ = q.shape
    n = L // HGRN_CHUNK

    def to_chunks(t):
        return jnp.moveaxis(t.reshape(B, n, HGRN_CHUNK, H, t.shape[-1]), 1, 0)

    lower = jnp.tril(jnp.ones((HGRN_CHUNK, HGRN_CHUNK), dtype=bool))[None, :, :, None, None]

    def step(S, inp):
        qc, kc, vc, gc = inp
        b = jnp.cumsum(gc, axis=1)
        o_inter = jnp.einsum('bthk,bhkv->bthv', qc * jnp.exp(b), S)
        diff = b[:, :, None] - b[:, None, :]
        decay = jnp.exp(jnp.where(lower, diff, -jnp.inf))
        scores = jnp.einsum('bthk,bshk,btshk->bhts', qc, kc, decay)
        o_intra = jnp.einsum('bhts,bshv->bthv', scores, vc)
        b_last = b[:, -1]
        S_new = jnp.exp(b_last)[..., None] * S + jnp.einsum(
            'bshk,bshv->bhkv', kc * jnp.exp(b_last[:, None] - b), vc)
        return S_new, o_inter + o_intra

    S_fin, o = lax.scan(step, s0, (to_chunks(q), to_chunks(k), to_chunks(v), to_chunks(log_f)))
    o = jnp.moveaxis(o, 0, 1).reshape(B, L, H, v.shape[-1])
    return S_fin, o


def _hgrn_mixer(p_ctx, p_lat, lb, norm_g):
    B = p_ctx['b_q'].shape[0]

    def heads(t):
        return t.astype(jnp.float32).reshape(t.shape[0], t.shape[1], HGRN_HEADS, HGRN_EXPAND)

    def ident(t):
        return t

    def rev(t):
        return t[:, ::-1]

    q_c, q_l = heads(jax.nn.silu(p_ctx['b_q'])), heads(jax.nn.silu(p_lat['b_q']))
    v_c, v_l = heads(p_ctx['b_i']), heads(p_lat['b_i'])
    s0 = jnp.zeros((B, HGRN_HEADS, HGRN_EXPAND, HGRN_EXPAND), jnp.float32)
    o_ctx = jnp.zeros_like(v_c)
    o_lat = jnp.zeros_like(v_l)
    for direction, (fname, order) in enumerate((('b_f_fwd', ident), ('b_f_bwd', rev))):
        lb_d = lb[direction].reshape(HGRN_HEADS, HGRN_EXPAND)
        f_c = lb_d + (1.0 - lb_d) * jax.nn.sigmoid(heads(p_ctx[fname]))
        f_l = lb_d + (1.0 - lb_d) * jax.nn.sigmoid(heads(p_lat[fname]))
        s_ctx, oc = _gla_chunk_scan(order(q_c), order(1.0 - f_c), order(v_c), order(jnp.log(f_c)), s0)
        _, ol = _gla_chunk_scan(order(q_l), order(1.0 - f_l), order(v_l), order(jnp.log(f_l)), s_ctx)
        o_ctx = o_ctx + order(oc)
        o_lat = o_lat + order(ol)
    g = norm_g.astype(jnp.float32).reshape(HGRN_HEADS, HGRN_EXPAND)

    def readout(o, p):
        o = o * lax.rsqrt(jnp.mean(o * o, axis=-1, keepdims=True) + EPS) * g
        o = o.reshape(o.shape[0], o.shape[1], HGRN_DIM) * jax.nn.silu(p['b_g'].astype(jnp.float32))
        return o.astype(p['b_g'].dtype)

    return readout(o_ctx, p_ctx), readout(o_lat, p_lat)


def _axial_rope(rows):
    row = jnp.repeat(jnp.arange(rows), GRID_W).astype(jnp.float32)
    col = (jnp.arange(rows * GRID_W) % GRID_W).astype(jnp.float32)
    inv = ROPE_THETA ** (-jnp.arange(0, ROPE_AXIS_DIM, 2, dtype=jnp.float32) / ROPE_AXIS_DIM)
    ar = row[:, None] * inv
    ac = col[:, None] * inv
    ang = jnp.concatenate([ar, ar, ac, ac], axis=-1)
    return jnp.cos(ang), jnp.sin(ang)


def _rotate_half(t):
    a, b = jnp.split(t, 2, axis=-1)
    return jnp.concatenate([-b, a], axis=-1)


def _apply_rope(x, cos, sin):
    xf = x.astype(jnp.float32)
    rot = jnp.concatenate([_rotate_half(xf[..., :ROPE_AXIS_DIM]), _rotate_half(xf[..., ROPE_AXIS_DIM:])], axis=-1)
    return (xf * cos[:, None] + rot * sin[:, None]).astype(x.dtype)


def _attn_heads(p, q_g, k_g):
    B, L, _ = p['c_q'].shape
    q = rms_norm(p['c_q'].reshape(B, L, ATT_HEADS, HEAD_DIM), q_g)
    k = rms_norm(p['c_k'].reshape(B, L, ATT_KV_HEADS, HEAD_DIM), k_g)
    v = p['c_v'].reshape(B, L, ATT_KV_HEADS, HEAD_DIM)
    return q, k, v


def _gqa(q, k, v):
    B, Q = q.shape[:2]
    qg = q.reshape(B, Q, ATT_KV_HEADS, ATT_GROUP, HEAD_DIM).astype(jnp.float32)
    s = jnp.einsum('bqkgd,bskd->bkgqs', qg, k.astype(jnp.float32)) * (HEAD_DIM ** -0.5)
    p = jax.nn.softmax(s, axis=-1)
    o = jnp.einsum('bkgqs,bskd->bqkgd', p, v.astype(jnp.float32))
    return o.reshape(B, Q, ATT_DIM).astype(q.dtype)


def _latent_attention(q_l, k_all, v_all):
    B, L = q_l.shape[:2]
    nb = L // Q_BLOCK
    qb = jnp.moveaxis(q_l.reshape(B, nb, Q_BLOCK, ATT_HEADS, HEAD_DIM), 1, 0)
    o = lax.map(lambda blk: _gqa(blk, k_all, v_all), qb)
    return jnp.moveaxis(o, 0, 1).reshape(B, L, ATT_DIM)


def _merge(p, ya, yb, yc, w_o):
    m = (jax.nn.sigmoid(p['gate_a']) * ya + jax.nn.sigmoid(p['gate_b']) * yb
         + jax.nn.sigmoid(p['gate_c']) * yc)
    return m @ w_o


def _route(h, router_w, router_b):
    T = h.shape[0]
    scores = jax.nn.sigmoid((h @ router_w).astype(jnp.float32))
    sel = scores + router_b.astype(jnp.float32)
    grp_score = lax.top_k(sel.reshape(T, N_GROUPS, EXPERTS_PER_GROUP), TOP_K)[0].sum(-1)
    best = jnp.argmax(grp_score, axis=-1)
    in_group = (jnp.arange(N_EXPERTS) // EXPERTS_PER_GROUP)[None, :] == best[:, None]
    _, idx = lax.top_k(jnp.where(in_group, sel, -jnp.inf), TOP_K)
    w = jnp.take_along_axis(scores, idx, axis=-1)
    return idx, w / jnp.sum(w, axis=-1, keepdims=True)


def _moe(h, router_w, router_b, w_gate, w_up, w_down):
    shp = h.shape
    hf = h.reshape(-1, shp[-1])
    T = hf.shape[0]
    idx, gw = _route(hf, router_w, router_b)
    A = T * TOP_K
    flat_e = idx.reshape(-1)
    flat_tok = jnp.repeat(jnp.arange(T, dtype=jnp.int32), TOP_K)
    flat_w = gw.reshape(-1)
    order = jnp.argsort(flat_e)
    se = flat_e[order]
    counts = jnp.bincount(flat_e, length=N_EXPERTS)
    padded = (counts + MOE_BLOCK - 1) // MOE_BLOCK * MOE_BLOCK
    start = jnp.cumsum(counts) - counts
    pend = jnp.cumsum(padded)
    pstart = pend - padded
    dest = pstart[se] + jnp.arange(A) - start[se]
    n_blocks = -(-A // MOE_BLOCK) + N_EXPERTS
    P = n_blocks * MOE_BLOCK
    row_tok = jnp.zeros((P,), jnp.int32).at[dest].set(flat_tok[order])
    row_w = jnp.zeros((P,), hf.dtype).at[dest].set(flat_w[order].astype(hf.dtype))
    blk_e = jnp.minimum(jnp.searchsorted(pend, jnp.arange(n_blocks) * MOE_BLOCK, side='right'), N_EXPERTS - 1)
    xs = hf[row_tok].reshape(n_blocks, MOE_BLOCK, shp[-1])

    def expert_rows(args):
        xb, e = args
        hid = jax.nn.silu(xb @ w_gate[e]) * (xb @ w_up[e])
        return hid @ w_down[e]

    ys = lax.map(expert_rows, (xs, blk_e)).reshape(P, shp[-1])
    out = jnp.zeros_like(hf).at[row_tok].add(ys * row_w[:, None])
    return out.reshape(shp)


def setup_inputs(seed: int = 0) -> dict:
    key = jax.random.key(seed)
    ks = jax.random.split(key, 24)
    f32 = jnp.float32

    def dense(k, shape, fan_in, gain=1.0):
        return jax.random.normal(k, shape, f32) * (gain * fan_in ** -0.5)

    def gain_vec(k, shape):
        return 1.0 + 0.1 * jax.random.normal(k, shape, f32)

    return {
        'x': jax.random.normal(ks[0], (BATCH, SEQ, D_MODEL), f32),
        'c': jax.random.normal(ks[1], (BATCH, D_MODEL), f32),
        'ctx': jax.random.normal(ks[2], (BATCH, CTX_LEN, D_MODEL), f32),
        'c_ctx': jax.random.normal(ks[3], (D_MODEL,), f32),
        'w_mod': dense(ks[4], (DEPTH, D_MODEL, 6 * D_MODEL), D_MODEL, 0.5),
        'b_mod': 0.02 * jax.random.normal(ks[5], (DEPTH, 6 * D_MODEL), f32),
        'norm1_g': gain_vec(ks[6], (DEPTH, D_MODEL)),
        'norm2_g': gain_vec(ks[7], (DEPTH, D_MODEL)),
        'w_in': dense(ks[8], (DEPTH, D_MODEL, W_IN_COLS), D_MODEL),
        'conv_w': dense(ks[9], (DEPTH, CONV_WIDTH, CONV_DIM), CONV_WIDTH),
        'lb_param': jax.random.normal(ks[10], (DEPTH, 2, HGRN_DIM), f32),
        'hgrn_norm_g': gain_vec(ks[11], (DEPTH, HGRN_DIM)),
        'q_norm_g': gain_vec(ks[12], (DEPTH, HEAD_DIM)),
        'k_norm_g': gain_vec(ks[13], (DEPTH, HEAD_DIM)),
        'w_a_out': dense(ks[14], (DEPTH, CONV_DIM, D_MODEL), CONV_DIM),
        'w_b_out': dense(ks[15], (DEPTH, HGRN_DIM, D_MODEL), HGRN_DIM),
        'w_c_out': dense(ks[16], (DEPTH, ATT_DIM, D_MODEL), ATT_DIM),
        'w_o': dense(ks[17], (DEPTH, D_MODEL, D_MODEL), D_MODEL),
        'router_w': dense(ks[18], (D_MODEL, N_EXPERTS), D_MODEL),
        'router_b': 0.01 * jax.random.normal(ks[19], (N_EXPERTS,), f32),
        'w_gate': dense(ks[20], (DEPTH, N_EXPERTS, D_MODEL, EXPERT_DFF), D_MODEL),
        'w_up': dense(ks[21], (DEPTH, N_EXPERTS, D_MODEL, EXPERT_DFF), D_MODEL),
        'w_down': dense(ks[22], (DEPTH, N_EXPERTS, EXPERT_DFF, D_MODEL), EXPERT_DFF),
        'final_g': gain_vec(ks[23], (D_MODEL,)),
    }


def reference(x, c, ctx, c_ctx, w_mod, b_mod, norm1_g, norm2_g, w_in, conv_w, lb_param,
              hgrn_norm_g, q_norm_g, k_norm_g, w_a_out, w_b_out, w_c_out, w_o,
              router_w, router_b, w_gate, w_up, w_down, final_g):
    L = x.shape[1]
    ROWS = L // GRID_W
    cos, sin = _axial_rope(ROWS)
    lower_bounds = _hgrn_lower_bounds(lb_param)
    h_lat, h_ctx = x, ctx
    for layer in range(DEPTH):
        last = layer == DEPTH - 1
        mod_lat = jnp.split((jax.nn.silu(c) @ w_mod[layer] + b_mod[layer])[:, None, :], 6, axis=-1)
        mod_ctx = jnp.split((jax.nn.silu(c_ctx) @ w_mod[layer] + b_mod[layer])[None, None, :], 6, axis=-1)
        sh1_l, sc1_l, g1_l, sh2_l, sc2_l, g2_l = mod_lat
        sh1_c, sc1_c, g1_c, sh2_c, sc2_c, g2_c = mod_ctx

        u_lat = rms_norm(h_lat, norm1_g[layer]) * (1.0 + sc1_l) + sh1_l
        u_ctx = rms_norm(h_ctx, norm1_g[layer]) * (1.0 + sc1_c) + sh1_c
        p_lat = _project(u_lat, w_in[layer])
        p_ctx = _project(u_ctx, w_in[layer])

        yb_ctx, yb_lat = _hgrn_mixer(p_ctx, p_lat, lower_bounds[layer], hgrn_norm_g[layer])

        q_c, k_c, v_c = _attn_heads(p_ctx, q_norm_g[layer], k_norm_g[layer])
        q_l, k_l, v_l = _attn_heads(p_lat, q_norm_g[layer], k_norm_g[layer])
        q_l = _apply_rope(q_l, cos, sin)
        k_l = _apply_rope(k_l, cos, sin)
        k_all = jnp.concatenate([k_c, k_l], axis=1)
        v_all = jnp.concatenate([v_c, v_l], axis=1)
        yc_lat = _latent_attention(q_l, k_all, v_all)

        ya_lat = _conv_mixer(p_lat, conv_w[layer], w_a_out[layer])
        out_lat = _merge(p_lat, ya_lat, yb_lat @ w_b_out[layer], yc_lat @ w_c_out[layer], w_o[layer])
        h_lat = h_lat + g1_l * out_lat
        if not last:
            ya_ctx = _conv_mixer(p_ctx, conv_w[layer], w_a_out[layer])
            yc_ctx = _gqa(q_c, k_c, v_c)
            out_ctx = _merge(p_ctx, ya_ctx, yb_ctx @ w_b_out[layer], yc_ctx @ w_c_out[layer], w_o[layer])
            h_ctx = h_ctx + g1_c * out_ctx

        m_lat = rms_norm(h_lat, norm2_g[layer]) * (1.0 + sc2_l) + sh2_l
        h_lat = h_lat + g2_l * _moe(m_lat, router_w, router_b, w_gate[layer], w_up[layer], w_down[layer])
        if not last:
            m_ctx = rms_norm(h_ctx, norm2_g[layer]) * (1.0 + sc2_c) + sh2_c
            h_ctx = h_ctx + g2_c * _moe(m_ctx, router_w, router_b, w_gate[layer], w_up[layer], w_down[layer])
    return rms_norm(h_lat, final_g)
```

```python
import functools

import jax
import jax.numpy as jnp
from jax import lax
from jax.experimental import pallas as pl
from jax.experimental.pallas import tpu as pltpu

F32 = jnp.float32
BF16 = jnp.bfloat16

D_MODEL = 1024
EPS = 1e-6
GRID_W = 64
CONV_DIM = 512
HGRN_DIM = 512
HGRN_HEADS = 4
HGRN_K = 128
SUB = 16
HCHUNK = 64
ATT_HEADS = 8
KV_HEADS = 2
ATT_GROUP = 4
HEAD_DIM = 64
ATT_DIM = 512
ROPE_THETA = 10000.0
ROPE_AXIS_DIM = 32
N_EXPERTS = 16
N_GROUPS = 4
EXPERTS_PER_GROUP = 4
EXPERT_DFF = 1024
MOE_BM = 256

TILE = 256
VMEM_LIMIT = 48 * 1024 * 1024

COL_GATE = 0
COL_A_VAL, COL_A_B, COL_A_C = 3072, 3584, 4096
COL_B_Q, COL_B_FF, COL_B_FB, COL_B_I, COL_B_G = 4608, 5120, 5632, 6144, 6656
COL_C_Q, COL_C_K4, COL_C_V4 = 7168, 7680, 8192
W_IN_COLS = 8704
IN_TN = 512


def _cparams(sem):
    return pltpu.CompilerParams(dimension_semantics=sem, vmem_limit_bytes=VMEM_LIMIT)


def _silu(x):
    return x * jax.nn.sigmoid(x)


def _mod_kernel(c_ref, w_ref, b_ref, o_ref):
    a = _silu(c_ref[...])
    o_ref[0] = jnp.dot(a, w_ref[0], precision=lax.Precision.HIGHEST,
                       preferred_element_type=F32) + b_ref[0]


def _modulation(cs, w_mod, b_mod):
    depth = w_mod.shape[0]
    ncol = w_mod.shape[2]
    tn = 1024
    return pl.pallas_call(
        _mod_kernel,
        out_shape=jax.ShapeDtypeStruct((depth, 8, ncol), F32),
        grid=(depth, ncol // tn),
        in_specs=[
            pl.BlockSpec((8, D_MODEL), lambda l, j: (0, 0)),
            pl.BlockSpec((1, D_MODEL, tn), lambda l, j: (l, 0, j)),
            pl.BlockSpec((1, 1, tn), lambda l, j: (l, 0, j)),
        ],
        out_specs=pl.BlockSpec((1, 8, tn), lambda l, j: (l, 0, j)),
        compiler_params=_cparams(("arbitrary", "arbitrary")),
        name="modulation",
    )(cs, w_mod, b_mod.reshape(depth, 1, ncol))


def _in_proj_kernel(h_ref, mod_ref, g_ref, w_ref, z_ref, u_ref, *, tm, s_all, ctx):
    i = pl.program_id(0)
    j = pl.program_id(1)

    @pl.when(j == 0)
    def _():
        x = h_ref[...]
        y = x * lax.rsqrt(jnp.mean(x * x, axis=-1, keepdims=True) + EPS) * g_ref[...]
        pos = (i * tm) % s_all + lax.broadcasted_iota(jnp.int32, (tm, 1), 0)
        is_ctx = pos < ctx
        sh = jnp.where(is_ctx, mod_ref[0, 0, 0:1, :], mod_ref[0, 1, 0:1, :])
        sc = jnp.where(is_ctx, mod_ref[0, 0, 1:2, :], mod_ref[0, 1, 1:2, :])
        u_ref[...] = (y * (1.0 + sc) + sh).astype(BF16)

    z_ref[...] = jnp.dot(u_ref[...], w_ref[...], preferred_element_type=F32).astype(BF16)


def _in_proj(h, modtab, g, w_in, s_all, ctx):
    n = h.shape[0]
    tm = next(t for t in (1280, 768, 512, 256) if s_all % t == 0)
    kern = functools.partial(_in_proj_kernel, tm=tm, s_all=s_all, ctx=ctx)
    return pl.pallas_call(
        kern,
        out_shape=jax.ShapeDtypeStruct((n, W_IN_COLS), BF16),
        grid=(n // tm, W_IN_COLS // IN_TN),
        in_specs=[
            pl.BlockSpec((tm, D_MODEL), lambda i, j: (i, 0)),
            pl.BlockSpec((1, 2, 6, D_MODEL), lambda i, j: ((i * tm) // s_all, 0, 0, 0)),
            pl.BlockSpec((1, D_MODEL), lambda i, j: (0, 0)),
            pl.BlockSpec((D_MODEL, IN_TN), lambda i, j: (0, j)),
        ],
        out_specs=pl.BlockSpec((tm, IN_TN), lambda i, j: (i, j)),
        scratch_shapes=[pltpu.VMEM((tm, D_MODEL), BF16)],
        compiler_params=_cparams(("arbitrary", "arbitrary")),
        name="in_proj",
    )(h, modtab, g, w_in)


def _seg_mean_sq(x, bd):
    sq = x * x
    hi = sq.astype(BF16)
    lo = (sq - hi.astype(F32)).astype(BF16)
    s = jnp.dot(hi, bd, preferred_element_type=F32) + jnp.dot(lo, bd, preferred_element_type=F32)
    return s * (1.0 / HEAD_DIM)


def _qk_kernel(q_ref, k_ref, cos_ref, sa_ref, sb_ref, qg_ref, kg_ref, bd_ref, qo_ref, ko_ref):
    bd = bd_ref[...]
    cos = cos_ref[...]
    sa = sa_ref[...]
    sb = sb_ref[...]

    def norm_rope(x, g):
        y = x * lax.rsqrt(_seg_mean_sq(x, bd) + EPS) * g
        up = pltpu.roll(y, ATT_DIM - 16, axis=1)
        dn = pltpu.roll(y, 16, axis=1)
        return y * cos + up * sa + dn * sb

    q = norm_rope(q_ref[...].astype(F32), qg_ref[...])
    qo_ref[...] = (q * (HEAD_DIM ** -0.5)).astype(BF16)
    ko_ref[...] = norm_rope(k_ref[...].astype(F32), kg_ref[...]).astype(BF16)


def _qk_prep(z, cos, sa, sb, qg, kg, bd, s_all):
    n = z.shape[0]
    nt = s_all // TILE
    cq, ck = COL_C_Q // ATT_DIM, COL_C_K4 // ATT_DIM
    tab = pl.BlockSpec((TILE, ATT_DIM), lambda i: (i % nt, 0))
    vec = pl.BlockSpec((1, ATT_DIM), lambda i: (0, 0))
    return pl.pallas_call(
        _qk_kernel,
        out_shape=(jax.ShapeDtypeStruct((n, ATT_DIM), BF16),) * 2,
        grid=(n // TILE,),
        in_specs=[
            pl.BlockSpec((TILE, ATT_DIM), lambda i: (i, cq)),
            pl.BlockSpec((TILE, ATT_DIM), lambda i: (i, ck)),
            tab, tab, tab, vec, vec,
            pl.BlockSpec((ATT_DIM, ATT_DIM), lambda i: (0, 0)),
        ],
        out_specs=(pl.BlockSpec((TILE, ATT_DIM), lambda i: (i, 0)),) * 2,
        compiler_params=_cparams(("arbitrary",)),
        name="qk_prep",
    )(z, z, cos, sa, sb, qg, kg, bd)


def _attn_kernel(q_ref, k_ref, v_ref, o_ref, qm_ref, m_ref, l_ref, acc_ref, *, tq, tk, nk, ctx):
    qi = pl.program_id(2)
    ki = pl.program_id(3)
    is_ctx = qi == 0
    lane_grp = lax.broadcasted_iota(jnp.int32, (tq, 256), 1) // HEAD_DIM

    @pl.when(ki == 0)
    def _():
        q = q_ref[...]
        for g in range(ATT_GROUP):
            qm_ref[g] = jnp.where(lane_grp == g, q, jnp.zeros_like(q))
        m_ref[...] = jnp.full(m_ref.shape, -jnp.inf, F32)
        l_ref[...] = jnp.zeros(l_ref.shape, F32)
        acc_ref[...] = jnp.zeros(acc_ref.shape, F32)

    def step(masked):
        k = k_ref[...]
        v = v_ref[...]
        alpha_full = jnp.zeros((tq, 256), F32)
        contrib = jnp.zeros((tq, 256), F32)
        for g in range(ATT_GROUP):
            s = lax.dot_general(qm_ref[g], k, (((1,), (1,)), ((), ())), preferred_element_type=F32)
            if masked:
                col = lax.broadcasted_iota(jnp.int32, (tq, tk), 1)
                s = jnp.where(col < ctx, s, -jnp.inf)
            m_old = m_ref[g]
            m_new = jnp.maximum(m_old, jnp.max(s, axis=1, keepdims=True))
            p = jnp.exp(s - m_new)
            alpha = jnp.exp(m_old - m_new)
            l_ref[g] = alpha * l_ref[g] + jnp.sum(p, axis=1, keepdims=True)
            m_ref[g] = m_new
            pv = jnp.dot(p.astype(BF16), v, preferred_element_type=F32)
            contrib = jnp.where(lane_grp == g, pv, contrib)
            alpha_full = jnp.where(lane_grp == g, alpha, alpha_full)
        acc_ref[...] = acc_ref[...] * alpha_full + contrib

    @pl.when(jnp.logical_and(is_ctx, ki == 0))
    def _():
        step(True)

    @pl.when(jnp.logical_not(is_ctx))
    def _():
        step(False)

    last = jnp.where(is_ctx, 0, nk - 1)

    @pl.when(ki == last)
    def _():
        l_full = jnp.zeros((tq, 256), F32)
        for g in range(ATT_GROUP):
            l_full = jnp.where(lane_grp == g, l_ref[g], l_full)
        o_ref[...] = (acc_ref[...] / l_full).astype(BF16)


def _attention(q_r, k4_r, z, batch, s_all, ctx):
    n = q_r.shape[0]
    tq = TILE
    tk = next(t for t in (1280, 768, 512, 256) if s_all % t == 0)
    nq, nk = s_all // tq, s_all // tk
    cv = COL_C_V4 // 256

    def kv_idx(ki, qi):
        return jnp.where(qi == 0, 0, ki)

    kern = functools.partial(_attn_kernel, tq=tq, tk=tk, nk=nk, ctx=ctx)
    return pl.pallas_call(
        kern,
        out_shape=jax.ShapeDtypeStruct((n, ATT_DIM), BF16),
        grid=(batch, KV_HEADS, nq, nk),
        in_specs=[
            pl.BlockSpec((tq, 256), lambda b, h, qi, ki: (b * nq + qi, h)),
            pl.BlockSpec((tk, 256), lambda b, h, qi, ki: (b * nk + kv_idx(ki, qi), h)),
            pl.BlockSpec((tk, 256), lambda b, h, qi, ki: (b * nk + kv_idx(ki, qi), cv + h)),
        ],
        out_specs=pl.BlockSpec((tq, 256), lambda b, h, qi, ki: (b * nq + qi, h)),
        scratch_shapes=[
            pltpu.VMEM((ATT_GROUP, tq, 256), BF16),
            pltpu.VMEM((ATT_GROUP, tq, 1), F32),
            pltpu.VMEM((ATT_GROUP, tq, 1), F32),
            pltpu.VMEM((tq, 256), F32),
        ],
        compiler_params=_cparams(("arbitrary",) * 4),
        name="attention",
    )(q_r, k4_r, z)


def _chunk_cumsum(g, reverse):
    rows = g.shape[0]
    pos = lax.broadcasted_iota(jnp.int32, (rows, 1), 0) % HCHUNK
    b = g
    s = 1
    while s < HCHUNK:
        if reverse:
            shifted = pltpu.roll(b, rows - s, axis=0)
            keep = pos < HCHUNK - s
        else:
            shifted = pltpu.roll(b, s, axis=0)
            keep = pos >= s
        b = b + jnp.where(keep, shifted, 0.0)
        s *= 2
    return b


def _hgrn_gates(f_raw, lb, reverse):
    fg = lb + (1.0 - lb) * jax.nn.sigmoid(f_raw.astype(F32))
    return 1.0 - fg, _chunk_cumsum(jnp.log(fg), reverse)


def _hgrn_state_kernel(ff_ref, fb_ref, vf_ref, vb_ref, lb_ref, sf_ref, sb_ref, stf_ref, stb_ref, *, nchunk):
    i = pl.program_id(2)

    @pl.when(i == 0)
    def _():
        stf_ref[...] = jnp.zeros(stf_ref.shape, F32)
        stb_ref[...] = jnp.zeros(stb_ref.shape, F32)

    for reverse, f_ref, v_ref, out_ref, st_ref in ((False, ff_ref, vf_ref, sf_ref, stf_ref),
                                                    (True, fb_ref, vb_ref, sb_ref, stb_ref)):
        lb = lb_ref[0, 1:2, :] if reverse else lb_ref[0, 0:1, :]
        kk, b = _hgrn_gates(f_ref[...], lb, reverse)
        v = v_ref[...]
        st = st_ref[...]
        order = range(nchunk - 1, -1, -1) if reverse else range(nchunk)
        for c in order:
            r0 = c * HCHUNK
            out_ref[0, 0, c] = st.astype(BF16)
            bc = b[r0:r0 + HCHUNK]
            b_last = bc[0:1] if reverse else bc[HCHUNK - 1:HCHUNK]
            kt = (kk[r0:r0 + HCHUNK] * jnp.exp(b_last - bc)).astype(BF16)
            ds = lax.dot_general(v[r0:r0 + HCHUNK], kt, (((0,), (0,)), ((), ())),
                                 preferred_element_type=F32)
            st = st * jnp.exp(b_last) + ds
        st_ref[...] = st


def _hgrn_states(z, lbt, batch, s_all):
    nt = s_all // TILE
    nchunk = TILE // HCHUNK
    cff, cfb, cv = COL_B_FF // HGRN_K, COL_B_FB // HGRN_K, COL_B_I // HGRN_K

    def rev(i):
        return jnp.where(i == 0, 0, nt - i)

    st_shape = jax.ShapeDtypeStruct((batch, HGRN_HEADS, nt * nchunk, HGRN_K, HGRN_K), BF16)
    kern = functools.partial(_hgrn_state_kernel, nchunk=nchunk)
    return pl.pallas_call(
        kern,
        out_shape=(st_shape, st_shape),
        grid=(batch, HGRN_HEADS, nt),
        in_specs=[
            pl.BlockSpec((TILE, HGRN_K), lambda b, h, i: (b * nt + i, cff + h)),
            pl.BlockSpec((TILE, HGRN_K), lambda b, h, i: (b * nt + rev(i), cfb + h)),
            pl.BlockSpec((TILE, HGRN_K), lambda b, h, i: (b * nt + i, cv + h)),
            pl.BlockSpec((TILE, HGRN_K), lambda b, h, i: (b * nt + rev(i), cv + h)),
            pl.BlockSpec((1, 2, HGRN_K), lambda b, h, i: (h, 0, 0)),
        ],
        out_specs=(
            pl.BlockSpec((1, 1, nchunk, HGRN_K, HGRN_K), lambda b, h, i: (b, h, i, 0, 0)),
            pl.BlockSpec((1, 1, nchunk, HGRN_K, HGRN_K), lambda b, h, i: (b, h, rev(i), 0, 0)),
        ),
        scratch_shapes=[pltpu.VMEM((HGRN_K, HGRN_K), F32), pltpu.VMEM((HGRN_K, HGRN_K), F32)],
        compiler_params=_cparams(("arbitrary",) * 3),
        name="hgrn_states",
    )(z, z, z, z, lbt)


def _hgrn_direction(qs, kk, b, v_bf, st_ref, reverse, nchunk):
    nsub = HCHUNK // SUB
    t_idx = lax.broadcasted_iota(jnp.int32, (SUB, 1), 0)
    v32 = v_bf.astype(F32)
    outs = []
    for c in range(nchunk):
        r0 = c * HCHUNK
        bc = b[r0:r0 + HCHUNK]
        qt = (qs[r0:r0 + HCHUNK] * jnp.exp(bc)).astype(BF16)
        o_inter = lax.dot_general(qt, st_ref[0, 0, c], (((1,), (1,)), ((), ())),
                                  preferred_element_type=F32)
        subs = []
        for i in range(nsub):
            a0 = r0 + i * SUB
            qi = qs[a0:a0 + SUB]
            bi = b[a0:a0 + SUB]
            o = o_inter[i * SUB:(i + 1) * SUB]
            for s in range(SUB):
                valid = (t_idx <= s) if reverse else (t_idx >= s)
                e = jnp.exp(jnp.where(valid, bi - b[a0 + s:a0 + s + 1], -jnp.inf))
                w = jnp.sum(qi * kk[a0 + s:a0 + s + 1] * e, axis=-1, keepdims=True)
                o = o + w * v32[a0 + s:a0 + s + 1]
            if reverse and i < nsub - 1:
                lo, hi, ref_row = a0 + SUB, r0 + HCHUNK, a0 + SUB
            elif (not reverse) and i > 0:
                lo, hi, ref_row = r0, a0, a0 - 1
            else:
                lo = hi = ref_row = None
            if lo is not None:
                b_ref = b[ref_row:ref_row + 1]
                q_sc = (qi * jnp.exp(bi - b_ref)).astype(BF16)
                k_sc = (kk[lo:hi] * jnp.exp(b_ref - b[lo:hi])).astype(BF16)
                sc = lax.dot_general(q_sc, k_sc, (((1,), (1,)), ((), ())), preferred_element_type=F32)
                o = o + jnp.dot(sc.astype(BF16), v_bf[lo:hi], preferred_element_type=F32)
            subs.append(o)
        outs.append(jnp.concatenate(subs, axis=0))
    return jnp.concatenate(outs, axis=0)


def _hgrn_out_kernel(q_ref, ff_ref, fb_ref, v_ref, gt_ref, sf_ref, sb_ref, lb_ref, ng_ref, y_ref, *, nchunk):
    qs = _silu(q_ref[...].astype(F32))
    v_bf = v_ref[...]
    o = jnp.zeros((TILE, HGRN_K), F32)
    for reverse, f_ref, st_ref in ((False, ff_ref, sf_ref), (True, fb_ref, sb_ref)):
        lb = lb_ref[0, 1:2, :] if reverse else lb_ref[0, 0:1, :]
        kk, b = _hgrn_gates(f_ref[...], lb, reverse)
        o = o + _hgrn_direction(qs, kk, b, v_bf, st_ref, reverse, nchunk)
    o = o * lax.rsqrt(jnp.mean(o * o, axis=-1, keepdims=True) + EPS) * ng_ref[0]
    y_ref[...] = (o * _silu(gt_ref[...].astype(F32))).astype(BF16)


def _hgrn_out(z, sf, sb, lbt, ngt, batch, s_all):
    n = z.shape[0]
    nt = s_all // TILE
    nchunk = TILE // HCHUNK
    cols = [c // HGRN_K for c in (COL_B_Q, COL_B_FF, COL_B_FB, COL_B_I, COL_B_G)]

    def zspec(c):
        return pl.BlockSpec((TILE, HGRN_K), lambda b, h, i: (b * nt + i, c + h))

    st_spec = pl.BlockSpec((1, 1, nchunk, HGRN_K, HGRN_K), lambda b, h, i: (b, h, i, 0, 0))
    kern = functools.partial(_hgrn_out_kernel, nchunk=nchunk)
    return pl.pallas_call(
        kern,
        out_shape=jax.ShapeDtypeStruct((n, HGRN_DIM), BF16),
        grid=(batch, HGRN_HEADS, nt),
        in_specs=[zspec(c) for c in cols] + [
            st_spec, st_spec,
            pl.BlockSpec((1, 2, HGRN_K), lambda b, h, i: (h, 0, 0)),
            pl.BlockSpec((1, 1, HGRN_K), lambda b, h, i: (h, 0, 0)),
        ],
        out_specs=pl.BlockSpec((TILE, HGRN_K), lambda b, h, i: (b * nt + i, h)),
        compiler_params=_cparams(("arbitrary",) * 3),
        name="hgrn_out",
    )(z, z, z, z, z, sf, sb, lbt, ngt)


def _route(logits_t, rb):
    scores = jax.nn.sigmoid(logits_t)
    sel = scores + rb
    t = logits_t.shape[1]
    rows = [sel[e:e + 1] for e in range(N_EXPERTS)]
    sc_rows = [scores[e:e + 1] for e in range(N_EXPERTS)]

    def group_score(gr):
        best = None
        for a in range(EXPERTS_PER_GROUP):
            for c in range(a + 1, EXPERTS_PER_GROUP):
                pair = gr[a] + gr[c]
                best = pair if best is None else jnp.maximum(best, pair)
        return best

    gs = [group_score(rows[g * 4:(g + 1) * 4]) for g in range(N_GROUPS)]
    best_g = jnp.zeros((1, t), jnp.int32)
    best_v = gs[0]
    for g in range(1, N_GROUPS):
        better = gs[g] > best_v
        best_g = jnp.where(better, g, best_g)
        best_v = jnp.where(better, gs[g], best_v)
    cand, craw = [], []
    for a in range(EXPERTS_PER_GROUP):
        cv, cr = rows[a], sc_rows[a]
        for g in range(1, N_GROUPS):
            cv = jnp.where(best_g == g, rows[g * 4 + a], cv)
            cr = jnp.where(best_g == g, sc_rows[g * 4 + a], cr)
        cand.append(cv)
        craw.append(cr)

    def argmax_first(vals):
        bi = jnp.zeros((1, t), jnp.int32)
        bv = vals[0]
        for a in range(1, EXPERTS_PER_GROUP):
            better = vals[a] > bv
            bi = jnp.where(better, a, bi)
            bv = jnp.where(better, vals[a], bv)
        return bi

    i1 = argmax_first(cand)
    i2 = argmax_first([jnp.where(i1 == a, -jnp.inf, cand[a]) for a in range(EXPERTS_PER_GROUP)])

    def pick(vals, idx):
        out = vals[0]
        for a in range(1, EXPERTS_PER_GROUP):
            out = jnp.where(idx == a, vals[a], out)
        return out

    w1, w2 = pick(craw, i1), pick(craw, i2)
    tot = w1 + w2
    return (best_g * 4 + i1, best_g * 4 + i2), (w1 / tot, w2 / tot)


def _merge_kernel(h_ref, ga_ref, gb_ref, gc_ref, av_ref, ab_ref, ac_ref,
                  avp_ref, acp_ref, avn_ref, acn_ref, yb_ref, yc_ref,
                  cw_ref, wa_ref, wb_ref, wc_ref, wo_ref, mod_ref, g2_ref, rw_ref, rb_ref,
                  h1_ref, m2_ref, ids_ref, wts_ref, *, nt):
    i = pl.program_id(0)
    ti = i % nt
    row = lax.broadcasted_iota(jnp.int32, (TILE, 1), 0)
    has_prev = ti > 1
    has_next = jnp.logical_and(ti > 0, ti < nt - 1)

    v = ac_ref[...].astype(F32) * av_ref[...].astype(F32)
    vp = acp_ref[15:16, :].astype(F32) * avp_ref[15:16, :].astype(F32)
    vn = acn_ref[0:1, :].astype(F32) * avn_ref[0:1, :].astype(F32)
    vp = jnp.where(has_prev, vp, 0.0)
    vn = jnp.where(has_next, vn, 0.0)
    v_prev = jnp.where(row == 0, vp, pltpu.roll(v, 1, axis=0))
    v_next = jnp.where(row == TILE - 1, vn, pltpu.roll(v, TILE - 1, axis=0))
    conv = v_prev * cw_ref[0:1, :] + v * cw_ref[1:2, :] + v_next * cw_ref[2:3, :]
    xa = (ab_ref[...].astype(F32) * conv).astype(BF16)

    ya = jnp.dot(xa, wa_ref[...], preferred_element_type=F32)
    yb = jnp.dot(yb_ref[...], wb_ref[...], preferred_element_type=F32)
    yc = jnp.dot(yc_ref[...], wc_ref[...], preferred_element_type=F32)
    m = (jax.nn.sigmoid(ga_ref[...].astype(F32)) * ya + jax.nn.sigmoid(gb_ref[...].astype(F32)) * yb
         + jax.nn.sigmoid(gc_ref[...].astype(F32)) * yc)
    out = jnp.dot(m.astype(BF16), wo_ref[...], preferred_element_type=F32)
    h1 = h_ref[...] + mod_ref[0, 0, 2:3, :] * out
    h1_ref[...] = h1

    y = h1 * lax.rsqrt(jnp.mean(h1 * h1, axis=-1, keepdims=True) + EPS) * g2_ref[...]
    m2 = y * (1.0 + mod_ref[0, 0, 4:5, :]) + mod_ref[0, 0, 3:4, :]
    m2_ref[...] = m2
    logits_t = lax.dot_general(rw_ref[...], m2, (((1,), (1,)), ((), ())),
                               precision=lax.Precision.HIGHEST, preferred_element_type=F32)
    ids, wts = _route(logits_t, rb_ref[...])
    for k in range(2):
        ids_ref[k:k + 1, :] = ids[k]
        wts_ref[k:k + 1, :] = wts[k]


def _merge(h, z, yb, yc, conv_w, wa, wb, wc, wo, modtab, g2, rw_t, rb, s_all):
    n = h.shape[0]
    nt = s_all // TILE
    hb = TILE // 16
    last16 = n // 16 - 1

    def zspec(width, col):
        return pl.BlockSpec((TILE, width), lambda i: (i, col // width))

    def halo(col, nxt):
        if nxt:
            return pl.BlockSpec((16, CONV_DIM), lambda i: (jnp.minimum((i + 1) * hb, last16), col // CONV_DIM))
        return pl.BlockSpec((16, CONV_DIM), lambda i: (jnp.maximum(i * hb - 1, 0), col // CONV_DIM))

    def full(shape):
        return pl.BlockSpec(shape, lambda i: (0,) * len(shape))

    kern = functools.partial(_merge_kernel, nt=nt)
    return pl.pallas_call(
        kern,
        out_shape=(
            jax.ShapeDtypeStruct((n, D_MODEL), F32),
            jax.ShapeDtypeStruct((n, D_MODEL), F32),
            jax.ShapeDtypeStruct((2, n), jnp.int32),
            jax.ShapeDtypeStruct((2, n), F32),
        ),
        grid=(n // TILE,),
        in_specs=[
            pl.BlockSpec((TILE, D_MODEL), lambda i: (i, 0)),
            zspec(D_MODEL, COL_GATE), zspec(D_MODEL, COL_GATE + D_MODEL), zspec(D_MODEL, COL_GATE + 2 * D_MODEL),
            zspec(CONV_DIM, COL_A_VAL), zspec(CONV_DIM, COL_A_B), zspec(CONV_DIM, COL_A_C),
            halo(COL_A_VAL, False), halo(COL_A_C, False), halo(COL_A_VAL, True), halo(COL_A_C, True),
            pl.BlockSpec((TILE, HGRN_DIM), lambda i: (i, 0)),
            pl.BlockSpec((TILE, ATT_DIM), lambda i: (i, 0)),
            full((3, CONV_DIM)),
            full((CONV_DIM, D_MODEL)), full((HGRN_DIM, D_MODEL)), full((ATT_DIM, D_MODEL)),
            full((D_MODEL, D_MODEL)),
            pl.BlockSpec((1, 1, 6, D_MODEL), lambda i: (i // nt, jnp.minimum(i % nt, 1), 0, 0)),
            full((1, D_MODEL)),
            full((N_EXPERTS, D_MODEL)),
            full((N_EXPERTS, 1)),
        ],
        out_specs=(
            pl.BlockSpec((TILE, D_MODEL), lambda i: (i, 0)),
            pl.BlockSpec((TILE, D_MODEL), lambda i: (i, 0)),
            pl.BlockSpec((2, TILE), lambda i: (0, i)),
            pl.BlockSpec((2, TILE), lambda i: (0, i)),
        ),
        compiler_params=_cparams(("arbitrary",)),
        name="merge_route",
    )(h, z, z, z, z, z, z, z, z, z, z, yb, yc, conv_w, wa, wb, wc, wo, modtab, g2, rw_t, rb)


def _dispatch_kernel(dest_ref, m2_hbm, xs_in, xs_hbm, dsm, sem_s, sem):
    del xs_in
    i = pl.program_id(0)
    cp = pltpu.make_async_copy(dest_ref.at[0, 0], dsm, sem_s)
    cp.start()
    cp.wait()

    def row_copy(t, k):
        return pltpu.make_async_copy(m2_hbm.at[pl.ds(i * TILE + t, 1)],
                                     xs_hbm.at[pl.ds(dsm[k * TILE + t], 1)], sem)

    def issue(t, carry):
        row_copy(t, 0).start()
        row_copy(t, 1).start()
        return carry

    lax.fori_loop(0, TILE, issue, 0)

    def drain(t, carry):
        row_copy(t, 0).wait()
        row_copy(t, 1).wait()
        return carry

    lax.fori_loop(0, TILE, drain, 0)


def _dispatch(dest_tiles, m2, xs_init):
    n = m2.shape[0]
    return pl.pallas_call(
        _dispatch_kernel,
        out_shape=jax.ShapeDtypeStruct(xs_init.shape, F32),
        grid=(n // TILE,),
        in_specs=[
            pl.BlockSpec((1, 1, 2 * TILE), lambda i: (i, 0, 0)),
            pl.BlockSpec(memory_space=pl.ANY),
            pl.BlockSpec(memory_space=pl.ANY),
        ],
        out_specs=pl.BlockSpec(memory_space=pl.ANY),
        scratch_shapes=[pltpu.SMEM((2 * TILE,), jnp.int32), pltpu.SemaphoreType.DMA, pltpu.SemaphoreType.DMA],
        input_output_aliases={2: 0},
        compiler_params=_cparams(("arbitrary",)),
        name="moe_dispatch",
    )(dest_tiles, m2, xs_init)


def _ffn_kernel(be_ref, bs_ref, nv_ref, x_ref, wg_ref, wu_ref, wd_ref, y_ref):
    del be_ref, bs_ref
    valid = pl.program_id(0) < nv_ref[0]

    @pl.when(valid)
    def _():
        x = x_ref[...].astype(BF16)
        hg = jnp.dot(x, wg_ref[0], preferred_element_type=F32)
        hu = jnp.dot(x, wu_ref[0], preferred_element_type=F32)
        hid = (_silu(hg) * hu).astype(BF16)
        y_ref[...] = jnp.dot(hid, wd_ref[0], preferred_element_type=F32)

    @pl.when(jnp.logical_not(valid))
    def _():
        y_ref[...] = jnp.zeros(y_ref.shape, F32)


def _expert_ffn(blk_e, blk_src, nvalid, xs, wg, wu, wd):
    p = xs.shape[0]
    nb = p // MOE_BM
    grid_spec = pltpu.PrefetchScalarGridSpec(
        num_scalar_prefetch=3,
        grid=(nb,),
        in_specs=[
            pl.BlockSpec((MOE_BM, D_MODEL), lambda i, be, bs, nv: (bs[i], 0)),
            pl.BlockSpec((1, D_MODEL, EXPERT_DFF), lambda i, be, bs, nv: (be[i], 0, 0)),
            pl.BlockSpec((1, D_MODEL, EXPERT_DFF), lambda i, be, bs, nv: (be[i], 0, 0)),
            pl.BlockSpec((1, EXPERT_DFF, D_MODEL), lambda i, be, bs, nv: (be[i], 0, 0)),
        ],
        out_specs=pl.BlockSpec((MOE_BM, D_MODEL), lambda i, be, bs, nv: (i, 0)),
    )
    return pl.pallas_call(
        _ffn_kernel,
        out_shape=jax.ShapeDtypeStruct((p, D_MODEL), F32),
        grid_spec=grid_spec,
        compiler_params=_cparams(("arbitrary",)),
        name="expert_ffn",
    )(blk_e, blk_src, nvalid, xs, wg, wu, wd)


def _combine_kernel(dest_ref, h_ref, w_ref, mod_ref, fg_ref, y_hbm, o_ref, dsm, ybuf, sem_s, sem, *, final):
    cp = pltpu.make_async_copy(dest_ref.at[0, 0], dsm, sem_s)
    cp.start()
    cp.wait()

    def row_copy(t, k):
        return pltpu.make_async_copy(y_hbm.at[pl.ds(dsm[k * TILE + t], 1)],
                                     ybuf.at[k, pl.ds(t, 1)], sem)

    def issue(t, carry):
        row_copy(t, 0).start()
        row_copy(t, 1).start()
        return carry

    lax.fori_loop(0, TILE, issue, 0)

    def drain(t, carry):
        row_copy(t, 0).wait()
        row_copy(t, 1).wait()
        return carry

    lax.fori_loop(0, TILE, drain, 0)

    w = w_ref[...]
    moe = w[:, 0:1] * ybuf[0] + w[:, 1:2] * ybuf[1]
    h2 = h_ref[...] + mod_ref[0, 0, 5:6, :] * moe
    if final:
        h2 = h2 * lax.rsqrt(jnp.mean(h2 * h2, axis=-1, keepdims=True) + EPS) * fg_ref[...]
    o_ref[...] = h2


def _combine(dest_tiles, h1, wts_col, modtab, final_g, y, batch, s_all, final):
    nt = s_all // TILE
    skip = 1 if final else 0
    nto = nt - skip

    def src(i):
        return (i // nto) * nt + i % nto + skip

    kern = functools.partial(_combine_kernel, final=final)
    return pl.pallas_call(
        kern,
        out_shape=jax.ShapeDtypeStruct((batch * nto * TILE, D_MODEL), F32),
        grid=(batch * nto,),
        in_specs=[
            pl.BlockSpec((1, 1, 2 * TILE), lambda i: (src(i), 0, 0)),
            pl.BlockSpec((TILE, D_MODEL), lambda i: (src(i), 0)),
            pl.BlockSpec((TILE, 2), lambda i: (src(i), 0)),
            pl.BlockSpec((1, 1, 6, D_MODEL), lambda i: (i // nto, jnp.minimum(i % nto + skip, 1), 0, 0)),
            pl.BlockSpec((1, D_MODEL), lambda i: (0, 0)),
            pl.BlockSpec(memory_space=pl.ANY),
        ],
        out_specs=pl.BlockSpec((TILE, D_MODEL), lambda i: (i, 0)),
        scratch_shapes=[
            pltpu.SMEM((2 * TILE,), jnp.int32),
            pltpu.VMEM((2, TILE, D_MODEL), F32),
            pltpu.SemaphoreType.DMA,
            pltpu.SemaphoreType.DMA,
        ],
        compiler_params=_cparams(("arbitrary",)),
        name="moe_combine",
    )(dest_tiles, h1, wts_col, modtab, final_g, y)


def _dispatch_plan(ids, n):
    flat_e = ids.reshape(-1)
    onehot = (flat_e[:, None] == jnp.arange(N_EXPERTS, dtype=jnp.int32)[None, :]).astype(jnp.int32)
    csum = jnp.cumsum(onehot, axis=0)
    counts = csum[-1]
    padded = (counts + MOE_BM - 1) // MOE_BM * MOE_BM
    pend = jnp.cumsum(padded)
    pstart = pend - padded
    dest = jnp.sum(onehot * (pstart[None, :] + csum - 1), axis=1).astype(jnp.int32)
    nb = -(-(2 * n) // MOE_BM) + N_EXPERTS
    nvalid = (pend[-1] // MOE_BM).astype(jnp.int32)
    blk = jnp.arange(nb, dtype=jnp.int32)
    blk_src = jnp.minimum(blk, nvalid - 1)
    blk_e = jnp.minimum(jnp.sum((blk_src[:, None] * MOE_BM >= pend[None, :]).astype(jnp.int32), axis=1),
                        N_EXPERTS - 1).astype(jnp.int32)
    dest2 = dest.reshape(2, n // TILE, TILE)
    dest_tiles = jnp.transpose(dest2, (1, 0, 2)).reshape(n // TILE, 1, 2 * TILE)
    return dest_tiles, blk_e, blk_src, nvalid.reshape(1), nb


def _reorder_w_in(w):
    a = w[:, 0:1536]
    b = w[:, 1536:4096]
    cq = w[:, 4096:4608]
    ck = w[:, 4608:4736]
    cv = w[:, 4736:4864]
    gates = w[:, 4864:7936]

    def rep4(t):
        return jnp.concatenate([t[:, 0:64]] * 4 + [t[:, 64:128]] * 4, axis=1)

    return jnp.concatenate([gates, a, b, cq, rep4(ck), rep4(cv)], axis=1).astype(BF16)


def _rope_tables(rows, ctx):
    row = jnp.repeat(jnp.arange(rows), GRID_W).astype(F32)
    col = (jnp.arange(rows * GRID_W) % GRID_W).astype(F32)
    inv = ROPE_THETA ** (-jnp.arange(0, ROPE_AXIS_DIM, 2, dtype=F32) / ROPE_AXIS_DIM)
    ar = row[:, None] * inv
    ac = col[:, None] * inv
    ang = jnp.concatenate([ar, ar, ac, ac], axis=-1)
    cos = jnp.concatenate([jnp.ones((ctx, HEAD_DIM), F32), jnp.cos(ang)], axis=0)
    sin = jnp.concatenate([jnp.zeros((ctx, HEAD_DIM), F32), jnp.sin(ang)], axis=0)
    first = (jnp.arange(HEAD_DIM) % ROPE_AXIS_DIM) < (ROPE_AXIS_DIM // 2)
    sa = jnp.where(first[None, :], -sin, 0.0)
    sb = jnp.where(first[None, :], 0.0, sin)
    rep = ATT_DIM // HEAD_DIM
    return jnp.tile(cos, (1, rep)), jnp.tile(sa, (1, rep)), jnp.tile(sb, (1, rep))


def kernel(x, c, ctx, c_ctx, w_mod, b_mod, norm1_g, norm2_g, w_in, conv_w, lb_param, hgrn_norm_g,
           q_norm_g, k_norm_g, w_a_out, w_b_out, w_c_out, w_o, router_w, router_b, w_gate, w_up,
           w_down, final_g):
    batch, seq, _ = x.shape
    n_ctx = ctx.shape[1]
    depth = w_mod.shape[0]
    assert n_ctx == TILE and seq % TILE == 0 and seq % GRID_W == 0 and batch + 1 <= 8
    s_all = n_ctx + seq
    n = batch * s_all

    h = jnp.concatenate([ctx, x], axis=1).reshape(n, D_MODEL)

    cs = jnp.zeros((8, D_MODEL), F32).at[:batch].set(c).at[batch].set(c_ctx)
    mods = _modulation(cs, w_mod, b_mod)
    lat = mods[:, :batch].reshape(depth, batch, 6, D_MODEL)
    cxt = jnp.broadcast_to(mods[:, batch].reshape(depth, 1, 6, D_MODEL), lat.shape)
    modtabs = jnp.stack([cxt, lat], axis=2)

    p = jax.nn.softmax(lb_param.astype(F32), axis=0)
    cum = jnp.cumsum(p, axis=0)
    lower = cum - cum[0:1]

    cos, sa, sb = _rope_tables(seq // GRID_W, n_ctx)
    seg = jnp.arange(ATT_DIM) // HEAD_DIM
    bd = (seg[:, None] == seg[None, :]).astype(BF16)
    rw_t = router_w.T.astype(F32)
    rb = router_b.reshape(N_EXPERTS, 1).astype(F32)
    fg = final_g.reshape(1, D_MODEL)

    out = None
    for layer in range(depth):
        modtab = modtabs[layer]
        z = _in_proj(h, modtab, norm1_g[layer].reshape(1, D_MODEL), _reorder_w_in(w_in[layer]), s_all, n_ctx)

        lbt = jnp.transpose(lower[layer].reshape(2, HGRN_HEADS, HGRN_K), (1, 0, 2))
        ngt = hgrn_norm_g[layer].reshape(HGRN_HEADS, 1, HGRN_K)
        sf, sbk = _hgrn_states(z, lbt, batch, s_all)
        yb = _hgrn_out(z, sf, sbk, lbt, ngt, batch, s_all)

        qg = jnp.tile(q_norm_g[layer], ATT_HEADS).reshape(1, ATT_DIM)
        kg = jnp.tile(k_norm_g[layer], ATT_HEADS).reshape(1, ATT_DIM)
        q_r, k4_r = _qk_prep(z, cos, sa, sb, qg, kg, bd, s_all)
        yc = _attention(q_r, k4_r, z, batch, s_all, n_ctx)

        h1, m2, ids, wts = _merge(
            h, z, yb, yc, conv_w[layer], w_a_out[layer].astype(BF16), w_b_out[layer].astype(BF16),
            w_c_out[layer].astype(BF16), w_o[layer].astype(BF16), modtab,
            norm2_g[layer].reshape(1, D_MODEL), rw_t, rb, s_all)

        dest_tiles, blk_e, blk_src, nvalid, nb = _dispatch_plan(ids, n)
        xs = _dispatch(dest_tiles, m2, jnp.zeros((nb * MOE_BM, D_MODEL), F32))
        y = _expert_ffn(blk_e, blk_src, nvalid, xs, w_gate[layer].astype(BF16), w_up[layer].astype(BF16),
                        w_down[layer].astype(BF16))
        wts_col = wts.T
        last = layer == depth - 1
        res = _combine(dest_tiles, h1, wts_col, modtab, fg, y, batch, s_all, last)
        if last:
            out = res.reshape(batch, seq, D_MODEL)
        else:
            h = res
    return out
```

```python
import functools

import jax
import jax.numpy as jnp
from jax import lax
from jax.experimental import pallas as pl
from jax.experimental.pallas import tpu as pltpu

F32 = jnp.float32
BF16 = jnp.bfloat16

D_MODEL = 1024
EPS = 1e-6
GRID_W = 64
CONV_DIM = 512
HGRN_DIM = 512
HGRN_HEADS = 4
HGRN_K = 128
SUB = 16
HCHUNK = 64
ATT_HEADS = 8
KV_HEADS = 2
ATT_GROUP = 4
HEAD_DIM = 64
ATT_DIM = 512
ROPE_THETA = 10000.0
ROPE_AXIS_DIM = 32
N_EXPERTS = 16
N_GROUPS = 4
EXPERTS_PER_GROUP = 4
EXPERT_DFF = 1024
MOE_BM = 256

TILE = 256
VMEM_LIMIT = 48 * 1024 * 1024

COL_GATE = 0
COL_A_VAL, COL_A_B, COL_A_C = 3072, 3584, 4096
COL_B_Q, COL_B_FF, COL_B_FB, COL_B_I, COL_B_G = 4608, 5120, 5632, 6144, 6656
COL_C_Q, COL_C_K, COL_C_V = 7168, 7680, 7808
W_IN_USED = 7936
W_IN_COLS = 8192
IN_TN = 512
KV_DIM = KV_HEADS * HEAD_DIM
VT_ROWS = HEAD_DIM + 16
LOG2E = 1.4426950408889634


def _cparams(sem):
    return pltpu.CompilerParams(dimension_semantics=sem, vmem_limit_bytes=VMEM_LIMIT)


def _silu(x):
    return x * jax.nn.sigmoid(x)


def _mod_kernel(c_ref, w_ref, b_ref, o_ref):
    a = _silu(c_ref[...])
    o_ref[0] = jnp.dot(a, w_ref[0], precision=lax.Precision.HIGHEST,
                       preferred_element_type=F32) + b_ref[0]


def _modulation(cs, w_mod, b_mod):
    depth = w_mod.shape[0]
    ncol = w_mod.shape[2]
    tn = 1024
    return pl.pallas_call(
        _mod_kernel,
        out_shape=jax.ShapeDtypeStruct((depth, 8, ncol), F32),
        grid=(depth, ncol // tn),
        in_specs=[
            pl.BlockSpec((8, D_MODEL), lambda l, j: (0, 0)),
            pl.BlockSpec((1, D_MODEL, tn), lambda l, j: (l, 0, j)),
            pl.BlockSpec((1, 1, tn), lambda l, j: (l, 0, j)),
        ],
        out_specs=pl.BlockSpec((1, 8, tn), lambda l, j: (l, 0, j)),
        compiler_params=_cparams(("arbitrary", "arbitrary")),
        name="modulation",
    )(cs, w_mod, b_mod.reshape(depth, 1, ncol))


def _in_proj_kernel(h_ref, mod_ref, g_ref, w_ref, z_ref, u_ref, *, tm, s_all, ctx):
    i = pl.program_id(0)
    j = pl.program_id(1)

    @pl.when(j == 0)
    def _():
        x = h_ref[...]
        y = x * lax.rsqrt(jnp.mean(x * x, axis=-1, keepdims=True) + EPS) * g_ref[...]
        pos = (i * tm) % s_all + lax.broadcasted_iota(jnp.int32, (tm, 1), 0)
        is_ctx = pos < ctx
        sh = jnp.where(is_ctx, mod_ref[0, 0, 0:1, :], mod_ref[0, 1, 0:1, :])
        sc = jnp.where(is_ctx, mod_ref[0, 0, 1:2, :], mod_ref[0, 1, 1:2, :])
        u_ref[...] = (y * (1.0 + sc) + sh).astype(BF16)

    z_ref[...] = jnp.dot(u_ref[...], w_ref[...], preferred_element_type=F32).astype(BF16)


def _in_proj(h, modtab, g, w_in, s_all, ctx):
    n = h.shape[0]
    tm = next(t for t in (1280, 768, 512, 256) if s_all % t == 0)
    kern = functools.partial(_in_proj_kernel, tm=tm, s_all=s_all, ctx=ctx)
    return pl.pallas_call(
        kern,
        out_shape=jax.ShapeDtypeStruct((n, W_IN_COLS), BF16),
        grid=(n // tm, W_IN_COLS // IN_TN),
        in_specs=[
            pl.BlockSpec((tm, D_MODEL), lambda i, j: (i, 0)),
            pl.BlockSpec((1, 2, 6, D_MODEL), lambda i, j: ((i * tm) // s_all, 0, 0, 0)),
            pl.BlockSpec((1, D_MODEL), lambda i, j: (0, 0)),
            pl.BlockSpec((D_MODEL, IN_TN), lambda i, j: (0, j)),
        ],
        out_specs=pl.BlockSpec((tm, IN_TN), lambda i, j: (i, j)),
        scratch_shapes=[pltpu.VMEM((tm, D_MODEL), BF16)],
        compiler_params=_cparams(("arbitrary", "arbitrary")),
        name="in_proj",
    )(h, modtab, g, w_in)


def _seg_mean_sq(x, bd):
    sq = x * x
    hi = sq.astype(BF16)
    lo = (sq - hi.astype(F32)).astype(BF16)
    s = jnp.dot(hi, bd, preferred_element_type=F32) + jnp.dot(lo, bd, preferred_element_type=F32)
    return s * (1.0 / HEAD_DIM)


def _qk_kernel(q_ref, k_ref, v_ref, cos_ref, sa_ref, sb_ref, qg_ref, kg_ref, bd_ref,
               qt_ref, ko_ref, vt_ref):
    bd = bd_ref[...]

    def norm_rope(x, g, width):
        y = x * lax.rsqrt(_seg_mean_sq(x, bd[0:width, 0:width]) + EPS) * g
        up = pltpu.roll(y, width - 16, axis=1)
        dn = pltpu.roll(y, 16, axis=1)
        return y * cos_ref[:, 0:width] + up * sa_ref[:, 0:width] + dn * sb_ref[:, 0:width]

    q = norm_rope(q_ref[...].astype(F32), qg_ref[...], ATT_DIM)
    qt_ref[...] = jnp.transpose(q * (HEAD_DIM ** -0.5 * LOG2E)).astype(BF16)
    k = norm_rope(k_ref[...].astype(F32), kg_ref[:, 0:KV_DIM], KV_DIM)
    for hd in range(KV_HEADS):
        ko_ref[hd] = k[:, hd * HEAD_DIM:(hd + 1) * HEAD_DIM].astype(BF16)
    vt = jnp.transpose(v_ref[...].astype(F32)).astype(BF16)
    for hd in range(KV_HEADS):
        vt_ref[hd, 0:HEAD_DIM, :] = vt[hd * HEAD_DIM:(hd + 1) * HEAD_DIM, :]
        vt_ref[hd, HEAD_DIM:VT_ROWS, :] = jnp.ones((VT_ROWS - HEAD_DIM, TILE), BF16)


def _qk_prep(z, cos, sa, sb, qg, kg, bd, s_all):
    n = z.shape[0]
    nt = s_all // TILE
    tab = pl.BlockSpec((TILE, ATT_DIM), lambda i: (i % nt, 0))
    vec = pl.BlockSpec((1, ATT_DIM), lambda i: (0, 0))
    return pl.pallas_call(
        _qk_kernel,
        out_shape=(
            jax.ShapeDtypeStruct((ATT_DIM, n), BF16),
            jax.ShapeDtypeStruct((KV_HEADS, n, HEAD_DIM), BF16),
            jax.ShapeDtypeStruct((KV_HEADS, VT_ROWS, n), BF16),
        ),
        grid=(n // TILE,),
        in_specs=[
            pl.BlockSpec((TILE, ATT_DIM), lambda i: (i, COL_C_Q // ATT_DIM)),
            pl.BlockSpec((TILE, KV_DIM), lambda i: (i, COL_C_K // KV_DIM)),
            pl.BlockSpec((TILE, KV_DIM), lambda i: (i, COL_C_V // KV_DIM)),
            tab, tab, tab, vec, vec,
            pl.BlockSpec((ATT_DIM, ATT_DIM), lambda i: (0, 0)),
        ],
        out_specs=(
            pl.BlockSpec((ATT_DIM, TILE), lambda i: (0, i)),
            pl.BlockSpec((KV_HEADS, TILE, HEAD_DIM), lambda i: (0, i, 0)),
            pl.BlockSpec((KV_HEADS, VT_ROWS, TILE), lambda i: (0, 0, i)),
        ),
        compiler_params=_cparams(("arbitrary",)),
        name="qk_prep",
    )(z, z, z, cos, sa, sb, qg, kg, bd)


def _attn_kernel(qt_ref, k_ref, vt_ref, o_ref, qs_ref, m_ref, acc_ref, *, tq, tk, nk, ctx):
    qi = pl.program_id(2)
    ki = pl.program_id(3)
    is_ctx = qi == 0

    @pl.when(ki == 0)
    def _():
        for g in range(ATT_GROUP):
            qs_ref[:, g * tq:(g + 1) * tq] = qt_ref[g * HEAD_DIM:(g + 1) * HEAD_DIM, :]
        m_ref[...] = jnp.full(m_ref.shape, -jnp.inf, F32)
        acc_ref[...] = jnp.zeros(acc_ref.shape, F32)

    def scores(j):
        k = k_ref[0, j * ctx:(j + 1) * ctx, :]
        return jnp.dot(k, qs_ref[...], preferred_element_type=F32)

    def update(j, s):
        m_old = m_ref[...]
        m_new = jnp.maximum(m_old, jnp.max(s, axis=0, keepdims=True))
        p = jnp.exp2(s - m_new)
        alpha = jnp.exp2(m_old - m_new)
        m_ref[...] = m_new
        pv = jnp.dot(vt_ref[0, :, j * ctx:(j + 1) * ctx], p.astype(BF16), preferred_element_type=F32)
        acc_ref[...] = acc_ref[...] * alpha + pv

    @pl.when(jnp.logical_and(is_ctx, ki == 0))
    def _():
        update(0, scores(0))

    @pl.when(jnp.logical_not(is_ctx))
    def _():
        nblk = tk // ctx
        ahead = 2
        pending = [scores(j) for j in range(min(ahead, nblk))]
        for j in range(nblk):
            if j + ahead < nblk:
                pending.append(scores(j + ahead))
            update(j, pending.pop(0))

    last = jnp.where(is_ctx, 0, nk - 1)

    @pl.when(ki == last)
    def _():
        out = acc_ref[0:HEAD_DIM, :] / acc_ref[HEAD_DIM:HEAD_DIM + 1, :]
        for g in range(ATT_GROUP):
            o_ref[g * HEAD_DIM:(g + 1) * HEAD_DIM, :] = out[:, g * tq:(g + 1) * tq].astype(BF16)


def _attention(q_t, k_hm, v_t, batch, s_all, ctx):
    n = q_t.shape[1]
    tq = TILE
    tk = next(t for t in (3328, 1280, 768, 512, 256) if s_all % t == 0)
    nq, nk = s_all // tq, s_all // tk
    grp = ATT_GROUP * HEAD_DIM

    def kv_idx(ki, qi):
        return jnp.where(qi == 0, 0, ki)

    kern = functools.partial(_attn_kernel, tq=tq, tk=tk, nk=nk, ctx=ctx)
    return pl.pallas_call(
        kern,
        out_shape=jax.ShapeDtypeStruct((ATT_DIM, n), BF16),
        grid=(batch, KV_HEADS, nq, nk),
        in_specs=[
            pl.BlockSpec((grp, tq), lambda b, h, qi, ki: (h, b * nq + qi)),
            pl.BlockSpec((1, tk, HEAD_DIM), lambda b, h, qi, ki: (h, b * nk + kv_idx(ki, qi), 0)),
            pl.BlockSpec((1, VT_ROWS, tk), lambda b, h, qi, ki: (h, 0, b * nk + kv_idx(ki, qi))),
        ],
        out_specs=pl.BlockSpec((grp, tq), lambda b, h, qi, ki: (h, b * nq + qi)),
        scratch_shapes=[
            pltpu.VMEM((HEAD_DIM, ATT_GROUP * tq), BF16),
            pltpu.VMEM((1, ATT_GROUP * tq), F32),
            pltpu.VMEM((VT_ROWS, ATT_GROUP * tq), F32),
        ],
        compiler_params=_cparams(("arbitrary",) * 4),
        name="attention",
    )(q_t, k_hm, v_t)


def _chunk_cumsum(g, reverse):
    rows = g.shape[0]
    pos = lax.broadcasted_iota(jnp.int32, (rows, 1), 0) % HCHUNK
    b = g
    s = 1
    while s < HCHUNK:
        if reverse:
            shifted = pltpu.roll(b, rows - s, axis=0)
            keep = pos < HCHUNK - s
        else:
            shifted = pltpu.roll(b, s, axis=0)
            keep = pos >= s
        b = b + jnp.where(keep, shifted, 0.0)
        s *= 2
    return b


def _hgrn_gates(f_raw, lb, reverse):
    fg = lb + (1.0 - lb) * jax.nn.sigmoid(f_raw.astype(F32))
    return 1.0 - fg, _chunk_cumsum(jnp.log(fg), reverse)


def _hgrn_state_kernel(ff_ref, fb_ref, vf_ref, vb_ref, lb_ref, sf_ref, sb_ref, stf_ref, stb_ref, *, nchunk):
    i = pl.program_id(2)

    @pl.when(i == 0)
    def _():
        stf_ref[...] = jnp.zeros(stf_ref.shape, F32)
        stb_ref[...] = jnp.zeros(stb_ref.shape, F32)

    for reverse, f_ref, v_ref, out_ref, st_ref in ((False, ff_ref, vf_ref, sf_ref, stf_ref),
                                                    (True, fb_ref, vb_ref, sb_ref, stb_ref)):
        lb = lb_ref[0, 1:2, :] if reverse else lb_ref[0, 0:1, :]
        kk, b = _hgrn_gates(f_ref[...], lb, reverse)
        v = v_ref[...]
        st = st_ref[...]
        order = range(nchunk - 1, -1, -1) if reverse else range(nchunk)
        for c in order:
            r0 = c * HCHUNK
            out_ref[0, 0, c] = st.astype(BF16)
            bc = b[r0:r0 + HCHUNK]
            b_last = bc[0:1] if reverse else bc[HCHUNK - 1:HCHUNK]
            kt = (kk[r0:r0 + HCHUNK] * jnp.exp(b_last - bc)).astype(BF16)
            ds = lax.dot_general(v[r0:r0 + HCHUNK], kt, (((0,), (0,)), ((), ())),
                                 preferred_element_type=F32)
            st = st * jnp.exp(b_last) + ds
        st_ref[...] = st


def _hgrn_states(z, lbt, batch, s_all):
    nt = s_all // TILE
    nchunk = TILE // HCHUNK
    cff, cfb, cv = COL_B_FF // HGRN_K, COL_B_FB // HGRN_K, COL_B_I // HGRN_K

    def rev(i):
        return jnp.where(i == 0, 0, nt - i)

    st_shape = jax.ShapeDtypeStruct((batch, HGRN_HEADS, nt * nchunk, HGRN_K, HGRN_K), BF16)
    kern = functools.partial(_hgrn_state_kernel, nchunk=nchunk)
    return pl.pallas_call(
        kern,
        out_shape=(st_shape, st_shape),
        grid=(batch, HGRN_HEADS, nt),
        in_specs=[
            pl.BlockSpec((TILE, HGRN_K), lambda b, h, i: (b * nt + i, cff + h)),
            pl.BlockSpec((TILE, HGRN_K), lambda b, h, i: (b * nt + rev(i), cfb + h)),
            pl.BlockSpec((TILE, HGRN_K), lambda b, h, i: (b * nt + i, cv + h)),
            pl.BlockSpec((TILE, HGRN_K), lambda b, h, i: (b * nt + rev(i), cv + h)),
            pl.BlockSpec((1, 2, HGRN_K), lambda b, h, i: (h, 0, 0)),
        ],
        out_specs=(
            pl.BlockSpec((1, 1, nchunk, HGRN_K, HGRN_K), lambda b, h, i: (b, h, i, 0, 0)),
            pl.BlockSpec((1, 1, nchunk, HGRN_K, HGRN_K), lambda b, h, i: (b, h, rev(i), 0, 0)),
        ),
        scratch_shapes=[pltpu.VMEM((HGRN_K, HGRN_K), F32), pltpu.VMEM((HGRN_K, HGRN_K), F32)],
        compiler_params=_cparams(("arbitrary",) * 3),
        name="hgrn_states",
    )(z, z, z, z, lbt)


def _hgrn_direction(qs, kk, b, v_bf, st_ref, reverse, nchunk):
    nsub = HCHUNK // SUB
    t_idx = lax.broadcasted_iota(jnp.int32, (SUB, 1), 0)
    v32 = v_bf.astype(F32)
    outs = []
    for c in range(nchunk):
        r0 = c * HCHUNK
        bc = b[r0:r0 + HCHUNK]
        qt = (qs[r0:r0 + HCHUNK] * jnp.exp(bc)).astype(BF16)
        o_inter = lax.dot_general(qt, st_ref[0, 0, c], (((1,), (1,)), ((), ())),
                                  preferred_element_type=F32)
        subs = []
        for i in range(nsub):
            a0 = r0 + i * SUB
            qi = qs[a0:a0 + SUB]
            bi = b[a0:a0 + SUB]
            o = o_inter[i * SUB:(i + 1) * SUB]
            for s in range(SUB):
                valid = (t_idx <= s) if reverse else (t_idx >= s)
                e = jnp.exp(jnp.where(valid, bi - b[a0 + s:a0 + s + 1], -jnp.inf))
                w = jnp.sum(qi * kk[a0 + s:a0 + s + 1] * e, axis=-1, keepdims=True)
                o = o + w * v32[a0 + s:a0 + s + 1]
            if reverse and i < nsub - 1:
                lo, hi, ref_row = a0 + SUB, r0 + HCHUNK, a0 + SUB
            elif (not reverse) and i > 0:
                lo, hi, ref_row = r0, a0, a0 - 1
            else:
                lo = hi = ref_row = None
            if lo is not None:
                b_ref = b[ref_row:ref_row + 1]
                q_sc = (qi * jnp.exp(bi - b_ref)).astype(BF16)
                k_sc = (kk[lo:hi] * jnp.exp(b_ref - b[lo:hi])).astype(BF16)
                sc = lax.dot_general(q_sc, k_sc, (((1,), (1,)), ((), ())), preferred_element_type=F32)
                o = o + jnp.dot(sc.astype(BF16), v_bf[lo:hi], preferred_element_type=F32)
            subs.append(o)
        outs.append(jnp.concatenate(subs, axis=0))
    return jnp.concatenate(outs, axis=0)


def _hgrn_out_kernel(q_ref, ff_ref, fb_ref, v_ref, gt_ref, sf_ref, sb_ref, lb_ref, ng_ref, y_ref, *, nchunk):
    qs = _silu(q_ref[...].astype(F32))
    v_bf = v_ref[...]
    o = jnp.zeros((TILE, HGRN_K), F32)
    for reverse, f_ref, st_ref in ((False, ff_ref, sf_ref), (True, fb_ref, sb_ref)):
        lb = lb_ref[0, 1:2, :] if reverse else lb_ref[0, 0:1, :]
        kk, b = _hgrn_gates(f_ref[...], lb, reverse)
        o = o + _hgrn_direction(qs, kk, b, v_bf, st_ref, reverse, nchunk)
    o = o * lax.rsqrt(jnp.mean(o * o, axis=-1, keepdims=True) + EPS) * ng_ref[0]
    y_ref[...] = (o * _silu(gt_ref[...].astype(F32))).astype(BF16)


def _hgrn_out(z, sf, sb, lbt, ngt, batch, s_all):
    n = z.shape[0]
    nt = s_all // TILE
    nchunk = TILE // HCHUNK
    cols = [c // HGRN_K for c in (COL_B_Q, COL_B_FF, COL_B_FB, COL_B_I, COL_B_G)]

    def zspec(c):
        return pl.BlockSpec((TILE, HGRN_K), lambda b, h, i: (b * nt + i, c + h))

    st_spec = pl.BlockSpec((1, 1, nchunk, HGRN_K, HGRN_K), lambda b, h, i: (b, h, i, 0, 0))
    kern = functools.partial(_hgrn_out_kernel, nchunk=nchunk)
    return pl.pallas_call(
        kern,
        out_shape=jax.ShapeDtypeStruct((n, HGRN_DIM), BF16),
        grid=(batch, HGRN_HEADS, nt),
        in_specs=[zspec(c) for c in cols] + [
            st_spec, st_spec,
            pl.BlockSpec((1, 2, HGRN_K), lambda b, h, i: (h, 0, 0)),
            pl.BlockSpec((1, 1, HGRN_K), lambda b, h, i: (h, 0, 0)),
        ],
        out_specs=pl.BlockSpec((TILE, HGRN_K), lambda b, h, i: (b * nt + i, h)),
        compiler_params=_cparams(("arbitrary",) * 3),
        name="hgrn_out",
    )(z, z, z, z, z, sf, sb, lbt, ngt)


def _route(logits_t, rb):
    scores = jax.nn.sigmoid(logits_t)
    sel = scores + rb
    t = logits_t.shape[1]
    rows = [sel[e:e + 1] for e in range(N_EXPERTS)]
    sc_rows = [scores[e:e + 1] for e in range(N_EXPERTS)]

    def group_score(gr):
        best = None
        for a in range(EXPERTS_PER_GROUP):
            for c in range(a + 1, EXPERTS_PER_GROUP):
                pair = gr[a] + gr[c]
                best = pair if best is None else jnp.maximum(best, pair)
        return best

    gs = [group_score(rows[g * 4:(g + 1) * 4]) for g in range(N_GROUPS)]
    best_g = jnp.zeros((1, t), jnp.int32)
    best_v = gs[0]
    for g in range(1, N_GROUPS):
        better = gs[g] > best_v
        best_g = jnp.where(better, g, best_g)
        best_v = jnp.where(better, gs[g], best_v)
    cand, craw = [], []
    for a in range(EXPERTS_PER_GROUP):
        cv, cr = rows[a], sc_rows[a]
        for g in range(1, N_GROUPS):
            cv = jnp.where(best_g == g, rows[g * 4 + a], cv)
            cr = jnp.where(best_g == g, sc_rows[g * 4 + a], cr)
        cand.append(cv)
        craw.append(cr)

    def argmax_first(vals):
        bi = jnp.zeros((1, t), jnp.int32)
        bv = vals[0]
        for a in range(1, EXPERTS_PER_GROUP):
            better = vals[a] > bv
            bi = jnp.where(better, a, bi)
            bv = jnp.where(better, vals[a], bv)
        return bi

    i1 = argmax_first(cand)
    i2 = argmax_first([jnp.where(i1 == a, -jnp.inf, cand[a]) for a in range(EXPERTS_PER_GROUP)])

    def pick(vals, idx):
        out = vals[0]
        for a in range(1, EXPERTS_PER_GROUP):
            out = jnp.where(idx == a, vals[a], out)
        return out

    w1, w2 = pick(craw, i1), pick(craw, i2)
    tot = w1 + w2
    return (best_g * 4 + i1, best_g * 4 + i2), (w1 / tot, w2 / tot)


def _merge_kernel(h_ref, ga_ref, gb_ref, gc_ref, av_ref, ab_ref, ac_ref,
                  avp_ref, acp_ref, avn_ref, acn_ref, yb_ref, yc_ref,
                  cw_ref, wa_ref, wb_ref, wc_ref, wo_ref, mod_ref, g2_ref, rw_ref, rb_ref,
                  h1_ref, m2_ref, ids_ref, wts_ref, *, nt):
    i = pl.program_id(0)
    ti = i % nt
    row = lax.broadcasted_iota(jnp.int32, (TILE, 1), 0)
    has_prev = ti > 1
    has_next = jnp.logical_and(ti > 0, ti < nt - 1)

    v = ac_ref[...].astype(F32) * av_ref[...].astype(F32)
    vp = acp_ref[15:16, :].astype(F32) * avp_ref[15:16, :].astype(F32)
    vn = acn_ref[0:1, :].astype(F32) * avn_ref[0:1, :].astype(F32)
    vp = jnp.where(has_prev, vp, 0.0)
    vn = jnp.where(has_next, vn, 0.0)
    v_prev = jnp.where(row == 0, vp, pltpu.roll(v, 1, axis=0))
    v_next = jnp.where(row == TILE - 1, vn, pltpu.roll(v, TILE - 1, axis=0))
    conv = v_prev * cw_ref[0:1, :] + v * cw_ref[1:2, :] + v_next * cw_ref[2:3, :]
    xa = (ab_ref[...].astype(F32) * conv).astype(BF16)

    ya = jnp.dot(xa, wa_ref[...], preferred_element_type=F32)
    yb = jnp.dot(yb_ref[...], wb_ref[...], preferred_element_type=F32)
    yc = lax.dot_general(yc_ref[...], wc_ref[...], (((0,), (0,)), ((), ())), preferred_element_type=F32)
    m = (jax.nn.sigmoid(ga_ref[...].astype(F32)) * ya + jax.nn.sigmoid(gb_ref[...].astype(F32)) * yb
         + jax.nn.sigmoid(gc_ref[...].astype(F32)) * yc)
    out = jnp.dot(m.astype(BF16), wo_ref[...], preferred_element_type=F32)
    h1 = h_ref[...] + mod_ref[0, 0, 2:3, :] * out
    h1_ref[...] = h1

    y = h1 * lax.rsqrt(jnp.mean(h1 * h1, axis=-1, keepdims=True) + EPS) * g2_ref[...]
    m2 = y * (1.0 + mod_ref[0, 0, 4:5, :]) + mod_ref[0, 0, 3:4, :]
    m2_ref[...] = m2
    logits_t = lax.dot_general(rw_ref[...], m2, (((1,), (1,)), ((), ())),
                               precision=lax.Precision.HIGHEST, preferred_element_type=F32)
    ids, wts = _route(logits_t, rb_ref[...])
    for k in range(2):
        ids_ref[k:k + 1, :] = ids[k]
        wts_ref[k:k + 1, :] = wts[k]


def _merge(h, z, yb, yc, conv_w, wa, wb, wc, wo, modtab, g2, rw_t, rb, s_all):
    n = h.shape[0]
    nt = s_all // TILE
    hb = TILE // 16
    last16 = n // 16 - 1

    def zspec(width, col):
        return pl.BlockSpec((TILE, width), lambda i: (i, col // width))

    def halo(col, nxt):
        if nxt:
            return pl.BlockSpec((16, CONV_DIM), lambda i: (jnp.minimum((i + 1) * hb, last16), col // CONV_DIM))
        return pl.BlockSpec((16, CONV_DIM), lambda i: (jnp.maximum(i * hb - 1, 0), col // CONV_DIM))

    def full(shape):
        return pl.BlockSpec(shape, lambda i: (0,) * len(shape))

    kern = functools.partial(_merge_kernel, nt=nt)
    return pl.pallas_call(
        kern,
        out_shape=(
            jax.ShapeDtypeStruct((n, D_MODEL), F32),
            jax.ShapeDtypeStruct((n, D_MODEL), F32),
            jax.ShapeDtypeStruct((2, n), jnp.int32),
            jax.ShapeDtypeStruct((2, n), F32),
        ),
        grid=(n // TILE,),
        in_specs=[
            pl.BlockSpec((TILE, D_MODEL), lambda i: (i, 0)),
            zspec(D_MODEL, COL_GATE), zspec(D_MODEL, COL_GATE + D_MODEL), zspec(D_MODEL, COL_GATE + 2 * D_MODEL),
            zspec(CONV_DIM, COL_A_VAL), zspec(CONV_DIM, COL_A_B), zspec(CONV_DIM, COL_A_C),
            halo(COL_A_VAL, False), halo(COL_A_C, False), halo(COL_A_VAL, True), halo(COL_A_C, True),
            pl.BlockSpec((TILE, HGRN_DIM), lambda i: (i, 0)),
            pl.BlockSpec((ATT_DIM, TILE), lambda i: (0, i)),
            full((3, CONV_DIM)),
            full((CONV_DIM, D_MODEL)), full((HGRN_DIM, D_MODEL)), full((ATT_DIM, D_MODEL)),
            full((D_MODEL, D_MODEL)),
            pl.BlockSpec((1, 1, 6, D_MODEL), lambda i: (i // nt, jnp.minimum(i % nt, 1), 0, 0)),
            full((1, D_MODEL)),
            full((N_EXPERTS, D_MODEL)),
            full((N_EXPERTS, 1)),
        ],
        out_specs=(
            pl.BlockSpec((TILE, D_MODEL), lambda i: (i, 0)),
            pl.BlockSpec((TILE, D_MODEL), lambda i: (i, 0)),
            pl.BlockSpec((2, TILE), lambda i: (0, i)),
            pl.BlockSpec((2, TILE), lambda i: (0, i)),
        ),
        compiler_params=_cparams(("arbitrary",)),
        name="merge_route",
    )(h, z, z, z, z, z, z, z, z, z, z, yb, yc, conv_w, wa, wb, wc, wo, modtab, g2, rw_t, rb)


def _dispatch_kernel(dest_ref, m2_ref, xs_in, xs_hbm, dsm, sem_s, sem):
    del xs_in
    cp = pltpu.make_async_copy(dest_ref.at[0, 0], dsm, sem_s)
    cp.start()
    cp.wait()

    def row_copy(t, k):
        return pltpu.make_async_copy(m2_ref.at[pl.ds(t, 1)],
                                     xs_hbm.at[pl.ds(dsm[k * TILE + t], 1)], sem)

    def issue(t, carry):
        row_copy(t, 0).start()
        row_copy(t, 1).start()
        return carry

    lax.fori_loop(0, TILE, issue, 0)

    def drain(t, carry):
        row_copy(t, 0).wait()
        row_copy(t, 1).wait()
        return carry

    lax.fori_loop(0, TILE, drain, 0)


def _dispatch(dest_tiles, m2, xs_init):
    n = m2.shape[0]
    return pl.pallas_call(
        _dispatch_kernel,
        out_shape=jax.ShapeDtypeStruct(xs_init.shape, F32),
        grid=(n // TILE,),
        in_specs=[
            pl.BlockSpec((1, 1, 2 * TILE), lambda i: (i, 0, 0)),
            pl.BlockSpec((TILE, D_MODEL), lambda i: (i, 0)),
            pl.BlockSpec(memory_space=pl.ANY),
        ],
        out_specs=pl.BlockSpec(memory_space=pl.ANY),
        scratch_shapes=[pltpu.SMEM((2 * TILE,), jnp.int32), pltpu.SemaphoreType.DMA, pltpu.SemaphoreType.DMA],
        input_output_aliases={2: 0},
        compiler_params=_cparams(("arbitrary",)),
        name="moe_dispatch",
    )(dest_tiles, m2, xs_init)


def _ffn_kernel(be_ref, bs_ref, nv_ref, x_ref, wg_ref, wu_ref, wd_ref, y_ref):
    del be_ref, bs_ref
    valid = pl.program_id(0) < nv_ref[0]

    @pl.when(valid)
    def _():
        x = x_ref[...].astype(BF16)
        hg = jnp.dot(x, wg_ref[0], preferred_element_type=F32)
        hu = jnp.dot(x, wu_ref[0], preferred_element_type=F32)
        hid = (_silu(hg) * hu).astype(BF16)
        y_ref[...] = jnp.dot(hid, wd_ref[0], preferred_element_type=F32)

    @pl.when(jnp.logical_not(valid))
    def _():
        y_ref[...] = jnp.zeros(y_ref.shape, F32)


def _expert_ffn(blk_e, blk_src, nvalid, xs, wg, wu, wd):
    p = xs.shape[0]
    nb = p // MOE_BM
    grid_spec = pltpu.PrefetchScalarGridSpec(
        num_scalar_prefetch=3,
        grid=(nb,),
        in_specs=[
            pl.BlockSpec((MOE_BM, D_MODEL), lambda i, be, bs, nv: (bs[i], 0)),
            pl.BlockSpec((1, D_MODEL, EXPERT_DFF), lambda i, be, bs, nv: (be[i], 0, 0)),
            pl.BlockSpec((1, D_MODEL, EXPERT_DFF), lambda i, be, bs, nv: (be[i], 0, 0)),
            pl.BlockSpec((1, EXPERT_DFF, D_MODEL), lambda i, be, bs, nv: (be[i], 0, 0)),
        ],
        out_specs=pl.BlockSpec((MOE_BM, D_MODEL), lambda i, be, bs, nv: (i, 0)),
    )
    return pl.pallas_call(
        _ffn_kernel,
        out_shape=jax.ShapeDtypeStruct((p, D_MODEL), F32),
        grid_spec=grid_spec,
        compiler_params=_cparams(("arbitrary",)),
        name="expert_ffn",
    )(blk_e, blk_src, nvalid, xs, wg, wu, wd)


def _combine_kernel(dest_ref, h_ref, w_ref, mod_ref, fg_ref, y_hbm, o_ref, dsm, ybuf, sem_s, sem, *, final):
    cp = pltpu.make_async_copy(dest_ref.at[0, 0], dsm, sem_s)
    cp.start()
    cp.wait()

    def row_copy(t, k):
        return pltpu.make_async_copy(y_hbm.at[pl.ds(dsm[k * TILE + t], 1)],
                                     ybuf.at[k, pl.ds(t, 1)], sem)

    def issue(t, carry):
        row_copy(t, 0).start()
        row_copy(t, 1).start()
        return carry

    lax.fori_loop(0, TILE, issue, 0)

    def drain(t, carry):
        row_copy(t, 0).wait()
        row_copy(t, 1).wait()
        return carry

    lax.fori_loop(0, TILE, drain, 0)

    w = w_ref[...]
    moe = w[:, 0:1] * ybuf[0] + w[:, 1:2] * ybuf[1]
    h2 = h_ref[...] + mod_ref[0, 0, 5:6, :] * moe
    if final:
        h2 = h2 * lax.rsqrt(jnp.mean(h2 * h2, axis=-1, keepdims=True) + EPS) * fg_ref[...]
    o_ref[...] = h2


def _combine(dest_tiles, h1, wts_col, modtab, final_g, y, batch, s_all, final):
    nt = s_all // TILE
    skip = 1 if final else 0
    nto = nt - skip

    def src(i):
        return (i // nto) * nt + i % nto + skip

    kern = functools.partial(_combine_kernel, final=final)
    return pl.pallas_call(
        kern,
        out_shape=jax.ShapeDtypeStruct((batch * nto * TILE, D_MODEL), F32),
        grid=(batch * nto,),
        in_specs=[
            pl.BlockSpec((1, 1, 2 * TILE), lambda i: (src(i), 0, 0)),
            pl.BlockSpec((TILE, D_MODEL), lambda i: (src(i), 0)),
            pl.BlockSpec((TILE, 2), lambda i: (src(i), 0)),
            pl.BlockSpec((1, 1, 6, D_MODEL), lambda i: (i // nto, jnp.minimum(i % nto + skip, 1), 0, 0)),
            pl.BlockSpec((1, D_MODEL), lambda i: (0, 0)),
            pl.BlockSpec(memory_space=pl.ANY),
        ],
        out_specs=pl.BlockSpec((TILE, D_MODEL), lambda i: (i, 0)),
        scratch_shapes=[
            pltpu.SMEM((2 * TILE,), jnp.int32),
            pltpu.VMEM((2, TILE, D_MODEL), F32),
            pltpu.SemaphoreType.DMA,
            pltpu.SemaphoreType.DMA,
        ],
        compiler_params=_cparams(("arbitrary",)),
        name="moe_combine",
    )(dest_tiles, h1, wts_col, modtab, final_g, y)


def _dispatch_plan(ids, n):
    flat_e = ids.reshape(-1)
    onehot = (flat_e[:, None] == jnp.arange(N_EXPERTS, dtype=jnp.int32)[None, :]).astype(jnp.int32)
    csum = jnp.cumsum(onehot, axis=0)
    counts = csum[-1]
    padded = (counts + MOE_BM - 1) // MOE_BM * MOE_BM
    pend = jnp.cumsum(padded)
    pstart = pend - padded
    dest = jnp.sum(onehot * (pstart[None, :] + csum - 1), axis=1).astype(jnp.int32)
    nb = -(-(2 * n) // MOE_BM) + N_EXPERTS
    nvalid = (pend[-1] // MOE_BM).astype(jnp.int32)
    blk = jnp.arange(nb, dtype=jnp.int32)
    blk_src = jnp.minimum(blk, nvalid - 1)
    blk_e = jnp.minimum(jnp.sum((blk_src[:, None] * MOE_BM >= pend[None, :]).astype(jnp.int32), axis=1),
                        N_EXPERTS - 1).astype(jnp.int32)
    dest2 = dest.reshape(2, n // TILE, TILE)
    dest_tiles = jnp.transpose(dest2, (1, 0, 2)).reshape(n // TILE, 1, 2 * TILE)
    return dest_tiles, blk_e, blk_src, nvalid.reshape(1), nb


def _reorder_w_in(w):
    rest = w[:, 0:4864]
    gates = w[:, 4864:W_IN_USED]
    pad = jnp.zeros((w.shape[0], W_IN_COLS - W_IN_USED), w.dtype)
    return jnp.concatenate([gates, rest, pad], axis=1).astype(BF16)


def _rope_tables(rows, ctx):
    row = jnp.repeat(jnp.arange(rows), GRID_W).astype(F32)
    col = (jnp.arange(rows * GRID_W) % GRID_W).astype(F32)
    inv = ROPE_THETA ** (-jnp.arange(0, ROPE_AXIS_DIM, 2, dtype=F32) / ROPE_AXIS_DIM)
    ar = row[:, None] * inv
    ac = col[:, None] * inv
    ang = jnp.concatenate([ar, ar, ac, ac], axis=-1)
    cos = jnp.concatenate([jnp.ones((ctx, HEAD_DIM), F32), jnp.cos(ang)], axis=0)
    sin = jnp.concatenate([jnp.zeros((ctx, HEAD_DIM), F32), jnp.sin(ang)], axis=0)
    first = (jnp.arange(HEAD_DIM) % ROPE_AXIS_DIM) < (ROPE_AXIS_DIM // 2)
    sa = jnp.where(first[None, :], -sin, 0.0)
    sb = jnp.where(first[None, :], 0.0, sin)
    rep = ATT_DIM // HEAD_DIM
    return jnp.tile(cos, (1, rep)), jnp.tile(sa, (1, rep)), jnp.tile(sb, (1, rep))


def kernel(x, c, ctx, c_ctx, w_mod, b_mod, norm1_g, norm2_g, w_in, conv_w, lb_param, hgrn_norm_g,
           q_norm_g, k_norm_g, w_a_out, w_b_out, w_c_out, w_o, router_w, router_b, w_gate, w_up,
           w_down, final_g):
    batch, seq, _ = x.shape
    n_ctx = ctx.shape[1]
    depth = w_mod.shape[0]
    assert n_ctx == TILE and seq % TILE == 0 and seq % GRID_W == 0 and batch + 1 <= 8
    s_all = n_ctx + seq
    n = batch * s_all

    h = jnp.concatenate([ctx, x], axis=1).reshape(n, D_MODEL)

    cs = jnp.zeros((8, D_MODEL), F32).at[:batch].set(c).at[batch].set(c_ctx)
    mods = _modulation(cs, w_mod, b_mod)
    lat = mods[:, :batch].reshape(depth, batch, 6, D_MODEL)
    cxt = jnp.broadcast_to(mods[:, batch].reshape(depth, 1, 6, D_MODEL), lat.shape)
    modtabs = jnp.stack([cxt, lat], axis=2)

    p = jax.nn.softmax(lb_param.astype(F32), axis=0)
    cum = jnp.cumsum(p, axis=0)
    lower = cum - cum[0:1]

    cos, sa, sb = _rope_tables(seq // GRID_W, n_ctx)
    seg = jnp.arange(ATT_DIM) // HEAD_DIM
    bd = (seg[:, None] == seg[None, :]).astype(BF16)
    rw_t = router_w.T.astype(F32)
    rb = router_b.reshape(N_EXPERTS, 1).astype(F32)
    fg = final_g.reshape(1, D_MODEL)

    out = None
    for layer in range(depth):
        modtab = modtabs[layer]
        z = _in_proj(h, modtab, norm1_g[layer].reshape(1, D_MODEL), _reorder_w_in(w_in[layer]), s_all, n_ctx)

        lbt = jnp.transpose(lower[layer].reshape(2, HGRN_HEADS, HGRN_K), (1, 0, 2))
        ngt = hgrn_norm_g[layer].reshape(HGRN_HEADS, 1, HGRN_K)
        sf, sbk = _hgrn_states(z, lbt, batch, s_all)
        yb = _hgrn_out(z, sf, sbk, lbt, ngt, batch, s_all)

        qg = jnp.tile(q_norm_g[layer], ATT_HEADS).reshape(1, ATT_DIM)
        kg = jnp.tile(k_norm_g[layer], ATT_HEADS).reshape(1, ATT_DIM)
        q_t, k_hm, v_t = _qk_prep(z, cos, sa, sb, qg, kg, bd, s_all)
        yc = _attention(q_t, k_hm, v_t, batch, s_all, n_ctx)

        h1, m2, ids, wts = _merge(
            h, z, yb, yc, conv_w[layer], w_a_out[layer].astype(BF16), w_b_out[layer].astype(BF16),
            w_c_out[layer].astype(BF16), w_o[layer].astype(BF16), modtab,
            norm2_g[layer].reshape(1, D_MODEL), rw_t, rb, s_all)

        dest_tiles, blk_e, blk_src, nvalid, nb = _dispatch_plan(ids, n)
        xs = _dispatch(dest_tiles, m2, jnp.zeros((nb * MOE_BM, D_MODEL), F32))
        y = _expert_ffn(blk_e, blk_src, nvalid, xs, w_gate[layer].astype(BF16), w_up[layer].astype(BF16),
                        w_down[layer].astype(BF16))
        wts_col = wts.T
        last = layer == depth - 1
        res = _combine(dest_tiles, h1, wts_col, modtab, fg, y, batch, s_all, last)
        if last:
            out = res.reshape(batch, seq, D_MODEL)
        else:
            h = res
    return out
```

```python
import functools

import jax
import jax.numpy as jnp
from jax import lax
from jax.experimental import pallas as pl
from jax.experimental.pallas import tpu as pltpu

F32 = jnp.float32
BF16 = jnp.bfloat16

D_MODEL = 1024
EPS = 1e-6
GRID_W = 64
CONV_DIM = 512
HGRN_DIM = 512
HGRN_HEADS = 4
HGRN_K = 128
SUB = 16
HCHUNK = 64
ATT_HEADS = 8
KV_HEADS = 2
ATT_GROUP = 4
HEAD_DIM = 64
ATT_DIM = 512
ROPE_THETA = 10000.0
ROPE_AXIS_DIM = 32
N_EXPERTS = 16
N_GROUPS = 4
EXPERTS_PER_GROUP = 4
EXPERT_DFF = 1024
MOE_BM = 256
DMA_UNROLL = 8

TILE = 256
VMEM_LIMIT = 48 * 1024 * 1024

COL_GATE = 0
COL_A_VAL, COL_A_B, COL_A_C = 3072, 3584, 4096
COL_B_Q, COL_B_FF, COL_B_FB, COL_B_I, COL_B_G = 4608, 5120, 5632, 6144, 6656
COL_C_Q, COL_C_K, COL_C_V = 7168, 7680, 7808
W_IN_USED = 7936
W_IN_COLS = 8192
IN_TN = 512
KV_DIM = KV_HEADS * HEAD_DIM
VT_ROWS = HEAD_DIM + 16
ATT_SAFE_LOG2 = 60.0
ATT_BOUND_MARGIN = 1.02
LOG2E = 1.4426950408889634


def _cparams(sem):
    return pltpu.CompilerParams(dimension_semantics=sem, vmem_limit_bytes=VMEM_LIMIT)


def _silu(x):
    return x * jax.nn.sigmoid(x)


def _mod_kernel(c_ref, w_ref, b_ref, o_ref):
    a = _silu(c_ref[...])
    o_ref[0] = jnp.dot(a, w_ref[0], precision=lax.Precision.HIGHEST,
                       preferred_element_type=F32) + b_ref[0]


def _modulation(cs, w_mod, b_mod):
    depth = w_mod.shape[0]
    ncol = w_mod.shape[2]
    tn = 1024
    return pl.pallas_call(
        _mod_kernel,
        out_shape=jax.ShapeDtypeStruct((depth, 8, ncol), F32),
        grid=(depth, ncol // tn),
        in_specs=[
            pl.BlockSpec((8, D_MODEL), lambda l, j: (0, 0)),
            pl.BlockSpec((1, D_MODEL, tn), lambda l, j: (l, 0, j)),
            pl.BlockSpec((1, 1, tn), lambda l, j: (l, 0, j)),
        ],
        out_specs=pl.BlockSpec((1, 8, tn), lambda l, j: (l, 0, j)),
        compiler_params=_cparams(("arbitrary", "arbitrary")),
        name="modulation",
    )(cs, w_mod, b_mod.reshape(depth, 1, ncol))


def _in_proj_kernel(h_ref, mod_ref, g_ref, w_ref, z_ref, u_ref, *, tm, s_all, ctx):
    i = pl.program_id(0)
    j = pl.program_id(1)

    @pl.when(j == 0)
    def _():
        x = h_ref[...]
        y = x * lax.rsqrt(jnp.mean(x * x, axis=-1, keepdims=True) + EPS) * g_ref[...]
        pos = (i * tm) % s_all + lax.broadcasted_iota(jnp.int32, (tm, 1), 0)
        is_ctx = pos < ctx
        sh = jnp.where(is_ctx, mod_ref[0, 0, 0:1, :], mod_ref[0, 1, 0:1, :])
        sc = jnp.where(is_ctx, mod_ref[0, 0, 1:2, :], mod_ref[0, 1, 1:2, :])
        u_ref[...] = (y * (1.0 + sc) + sh).astype(BF16)

    z_ref[...] = jnp.dot(u_ref[...], w_ref[...], preferred_element_type=F32).astype(BF16)


def _in_proj(h, modtab, g, w_in, s_all, ctx):
    n = h.shape[0]
    tm = next(t for t in (1280, 768, 512, 256) if s_all % t == 0)
    kern = functools.partial(_in_proj_kernel, tm=tm, s_all=s_all, ctx=ctx)
    return pl.pallas_call(
        kern,
        out_shape=jax.ShapeDtypeStruct((n, W_IN_COLS), BF16),
        grid=(n // tm, W_IN_COLS // IN_TN),
        in_specs=[
            pl.BlockSpec((tm, D_MODEL), lambda i, j: (i, 0)),
            pl.BlockSpec((1, 2, 6, D_MODEL), lambda i, j: ((i * tm) // s_all, 0, 0, 0)),
            pl.BlockSpec((1, D_MODEL), lambda i, j: (0, 0)),
            pl.BlockSpec((D_MODEL, IN_TN), lambda i, j: (0, j)),
        ],
        out_specs=pl.BlockSpec((tm, IN_TN), lambda i, j: (i, j)),
        scratch_shapes=[pltpu.VMEM((tm, D_MODEL), BF16)],
        compiler_params=_cparams(("arbitrary", "arbitrary")),
        name="in_proj",
    )(h, modtab, g, w_in)


def _seg_mean_sq(x, bd):
    sq = x * x
    hi = sq.astype(BF16)
    lo = (sq - hi.astype(F32)).astype(BF16)
    s = jnp.dot(hi, bd, preferred_element_type=F32) + jnp.dot(lo, bd, preferred_element_type=F32)
    return s * (1.0 / HEAD_DIM)


def _qk_kernel(q_ref, k_ref, v_ref, cos_ref, sa_ref, sb_ref, qg_ref, kg_ref, bd_ref,
               qt_ref, ko_ref, vt_ref):
    bd = bd_ref[...]

    def norm_rope(x, g, width):
        y = x * lax.rsqrt(_seg_mean_sq(x, bd[0:width, 0:width]) + EPS) * g
        up = pltpu.roll(y, width - 16, axis=1)
        dn = pltpu.roll(y, 16, axis=1)
        return y * cos_ref[:, 0:width] + up * sa_ref[:, 0:width] + dn * sb_ref[:, 0:width]

    q = norm_rope(q_ref[...].astype(F32), qg_ref[...], ATT_DIM)
    qt_ref[...] = jnp.transpose(q * (HEAD_DIM ** -0.5 * LOG2E)).astype(BF16)
    k = norm_rope(k_ref[...].astype(F32), kg_ref[:, 0:KV_DIM], KV_DIM)
    for hd in range(KV_HEADS):
        ko_ref[hd] = k[:, hd * HEAD_DIM:(hd + 1) * HEAD_DIM].astype(BF16)
    vt = jnp.transpose(v_ref[...].astype(F32)).astype(BF16)
    for hd in range(KV_HEADS):
        vt_ref[hd, 0:HEAD_DIM, :] = vt[hd * HEAD_DIM:(hd + 1) * HEAD_DIM, :]
        vt_ref[hd, HEAD_DIM:VT_ROWS, :] = jnp.ones((VT_ROWS - HEAD_DIM, TILE), BF16)


def _qk_prep(z, cos, sa, sb, qg, kg, bd, s_all):
    n = z.shape[0]
    nt = s_all // TILE
    tab = pl.BlockSpec((TILE, ATT_DIM), lambda i: (i % nt, 0))
    vec = pl.BlockSpec((1, ATT_DIM), lambda i: (0, 0))
    return pl.pallas_call(
        _qk_kernel,
        out_shape=(
            jax.ShapeDtypeStruct((ATT_DIM, n), BF16),
            jax.ShapeDtypeStruct((KV_HEADS, n, HEAD_DIM), BF16),
            jax.ShapeDtypeStruct((KV_HEADS, VT_ROWS, n), BF16),
        ),
        grid=(n // TILE,),
        in_specs=[
            pl.BlockSpec((TILE, ATT_DIM), lambda i: (i, COL_C_Q // ATT_DIM)),
            pl.BlockSpec((TILE, KV_DIM), lambda i: (i, COL_C_K // KV_DIM)),
            pl.BlockSpec((TILE, KV_DIM), lambda i: (i, COL_C_V // KV_DIM)),
            tab, tab, tab, vec, vec,
            pl.BlockSpec((ATT_DIM, ATT_DIM), lambda i: (0, 0)),
        ],
        out_specs=(
            pl.BlockSpec((ATT_DIM, TILE), lambda i: (0, i)),
            pl.BlockSpec((KV_HEADS, TILE, HEAD_DIM), lambda i: (0, i, 0)),
            pl.BlockSpec((KV_HEADS, VT_ROWS, TILE), lambda i: (0, 0, i)),
        ),
        compiler_params=_cparams(("arbitrary",)),
        name="qk_prep",
    )(z, z, z, cos, sa, sb, qg, kg, bd)


def _attn_kernel(qt_ref, k_ref, vt_ref, o_ref, qs_ref, m_ref, acc_ref, *, tq, tk, nk, ctx, online):
    qi = pl.program_id(2)
    ki = pl.program_id(3)
    is_ctx = qi == 0

    @pl.when(ki == 0)
    def _():
        for g in range(ATT_GROUP):
            qs_ref[:, g * tq:(g + 1) * tq] = qt_ref[g * HEAD_DIM:(g + 1) * HEAD_DIM, :]
        m_ref[...] = jnp.full(m_ref.shape, -jnp.inf, F32)
        acc_ref[...] = jnp.zeros(acc_ref.shape, F32)

    def scores(j):
        k = k_ref[0, j * ctx:(j + 1) * ctx, :]
        return jnp.dot(k, qs_ref[...], preferred_element_type=F32)

    def update(j, s):
        vt = vt_ref[0, :, j * ctx:(j + 1) * ctx]
        if online:
            m_old = m_ref[...]
            m_new = jnp.maximum(m_old, jnp.max(s, axis=0, keepdims=True))
            p = jnp.exp2(s - m_new)
            alpha = jnp.exp2(m_old - m_new)
            m_ref[...] = m_new
            pv = jnp.dot(vt, p.astype(BF16), preferred_element_type=F32)
            acc_ref[...] = acc_ref[...] * alpha + pv
        else:
            pv = jnp.dot(vt, jnp.exp2(s).astype(BF16), preferred_element_type=F32)
            acc_ref[...] = acc_ref[...] + pv

    @pl.when(jnp.logical_and(is_ctx, ki == 0))
    def _():
        update(0, scores(0))

    @pl.when(jnp.logical_not(is_ctx))
    def _():
        nblk = tk // ctx
        ahead = 2
        pending = [scores(j) for j in range(min(ahead, nblk))]
        for j in range(nblk):
            if j + ahead < nblk:
                pending.append(scores(j + ahead))
            update(j, pending.pop(0))

    last = jnp.where(is_ctx, 0, nk - 1)

    @pl.when(ki == last)
    def _():
        out = acc_ref[0:HEAD_DIM, :] / acc_ref[HEAD_DIM:HEAD_DIM + 1, :]
        for g in range(ATT_GROUP):
            o_ref[g * HEAD_DIM:(g + 1) * HEAD_DIM, :] = out[:, g * tq:(g + 1) * tq].astype(BF16)


def _attention(q_t, k_hm, v_t, batch, s_all, ctx, online):
    n = q_t.shape[1]
    tq = TILE
    tk = next(t for t in (3328, 1280, 768, 512, 256) if s_all % t == 0)
    nq, nk = s_all // tq, s_all // tk
    grp = ATT_GROUP * HEAD_DIM

    def kv_idx(ki, qi):
        return jnp.where(qi == 0, 0, ki)

    kern = functools.partial(_attn_kernel, tq=tq, tk=tk, nk=nk, ctx=ctx, online=online)
    return pl.pallas_call(
        kern,
        out_shape=jax.ShapeDtypeStruct((ATT_DIM, n), BF16),
        grid=(batch, KV_HEADS, nq, nk),
        in_specs=[
            pl.BlockSpec((grp, tq), lambda b, h, qi, ki: (h, b * nq + qi)),
            pl.BlockSpec((1, tk, HEAD_DIM), lambda b, h, qi, ki: (h, b * nk + kv_idx(ki, qi), 0)),
            pl.BlockSpec((1, VT_ROWS, tk), lambda b, h, qi, ki: (h, 0, b * nk + kv_idx(ki, qi))),
        ],
        out_specs=pl.BlockSpec((grp, tq), lambda b, h, qi, ki: (h, b * nq + qi)),
        scratch_shapes=[
            pltpu.VMEM((HEAD_DIM, ATT_GROUP * tq), BF16),
            pltpu.VMEM((1, ATT_GROUP * tq), F32),
            pltpu.VMEM((VT_ROWS, ATT_GROUP * tq), F32),
        ],
        compiler_params=_cparams(("arbitrary",) * 4),
        name="attention",
    )(q_t, k_hm, v_t)


def _chunk_cumsum(g, reverse):
    rows = g.shape[0]
    pos = lax.broadcasted_iota(jnp.int32, (rows, 1), 0) % HCHUNK
    b = g
    s = 1
    while s < HCHUNK:
        if reverse:
            shifted = pltpu.roll(b, rows - s, axis=0)
            keep = pos < HCHUNK - s
        else:
            shifted = pltpu.roll(b, s, axis=0)
            keep = pos >= s
        b = b + jnp.where(keep, shifted, 0.0)
        s *= 2
    return b


def _hgrn_gates(f_raw, lb, reverse):
    fg = lb + (1.0 - lb) * jax.nn.sigmoid(f_raw.astype(F32))
    return 1.0 - fg, _chunk_cumsum(jnp.log(fg), reverse)


def _hgrn_state_kernel(ff_ref, fb_ref, vf_ref, vb_ref, lb_ref, sf_ref, sb_ref, stf_ref, stb_ref, *, nchunk):
    i = pl.program_id(2)

    @pl.when(i == 0)
    def _():
        stf_ref[...] = jnp.zeros(stf_ref.shape, F32)
        stb_ref[...] = jnp.zeros(stb_ref.shape, F32)

    for reverse, f_ref, v_ref, out_ref, st_ref in ((False, ff_ref, vf_ref, sf_ref, stf_ref),
                                                    (True, fb_ref, vb_ref, sb_ref, stb_ref)):
        lb = lb_ref[0, 1:2, :] if reverse else lb_ref[0, 0:1, :]
        kk, b = _hgrn_gates(f_ref[...], lb, reverse)
        v = v_ref[...]
        st = st_ref[...]
        order = range(nchunk - 1, -1, -1) if reverse else range(nchunk)
        for c in order:
            r0 = c * HCHUNK
            out_ref[0, 0, c] = st.astype(BF16)
            bc = b[r0:r0 + HCHUNK]
            b_last = bc[0:1] if reverse else bc[HCHUNK - 1:HCHUNK]
            kt = (kk[r0:r0 + HCHUNK] * jnp.exp(b_last - bc)).astype(BF16)
            ds = lax.dot_general(v[r0:r0 + HCHUNK], kt, (((0,), (0,)), ((), ())),
                                 preferred_element_type=F32)
            st = st * jnp.exp(b_last) + ds
        st_ref[...] = st


def _hgrn_states(z, lbt, batch, s_all):
    nt = s_all // TILE
    nchunk = TILE // HCHUNK
    cff, cfb, cv = COL_B_FF // HGRN_K, COL_B_FB // HGRN_K, COL_B_I // HGRN_K

    def rev(i):
        return jnp.where(i == 0, 0, nt - i)

    st_shape = jax.ShapeDtypeStruct((batch, HGRN_HEADS, nt * nchunk, HGRN_K, HGRN_K), BF16)
    kern = functools.partial(_hgrn_state_kernel, nchunk=nchunk)
    return pl.pallas_call(
        kern,
        out_shape=(st_shape, st_shape),
        grid=(batch, HGRN_HEADS, nt),
        in_specs=[
            pl.BlockSpec((TILE, HGRN_K), lambda b, h, i: (b * nt + i, cff + h)),
            pl.BlockSpec((TILE, HGRN_K), lambda b, h, i: (b * nt + rev(i), cfb + h)),
            pl.BlockSpec((TILE, HGRN_K), lambda b, h, i: (b * nt + i, cv + h)),
            pl.BlockSpec((TILE, HGRN_K), lambda b, h, i: (b * nt + rev(i), cv + h)),
            pl.BlockSpec((1, 2, HGRN_K), lambda b, h, i: (h, 0, 0)),
        ],
        out_specs=(
            pl.BlockSpec((1, 1, nchunk, HGRN_K, HGRN_K), lambda b, h, i: (b, h, i, 0, 0)),
            pl.BlockSpec((1, 1, nchunk, HGRN_K, HGRN_K), lambda b, h, i: (b, h, rev(i), 0, 0)),
        ),
        scratch_shapes=[pltpu.VMEM((HGRN_K, HGRN_K), F32), pltpu.VMEM((HGRN_K, HGRN_K), F32)],
        compiler_params=_cparams(("arbitrary",) * 3),
        name="hgrn_states",
    )(z, z, z, z, lbt)


def _hgrn_direction(qs, kk, b, v_bf, st_ref, reverse, nchunk):
    nsub = HCHUNK // SUB
    t_idx = lax.broadcasted_iota(jnp.int32, (SUB, 1), 0)
    v32 = v_bf.astype(F32)
    outs = []
    for c in range(nchunk):
        r0 = c * HCHUNK
        bc = b[r0:r0 + HCHUNK]
        qt = (qs[r0:r0 + HCHUNK] * jnp.exp(bc)).astype(BF16)
        o_inter = lax.dot_general(qt, st_ref[0, 0, c], (((1,), (1,)), ((), ())),
                                  preferred_element_type=F32)
        subs = []
        for i in range(nsub):
            a0 = r0 + i * SUB
            qi = qs[a0:a0 + SUB]
            bi = b[a0:a0 + SUB]
            o = o_inter[i * SUB:(i + 1) * SUB]
            for s in range(SUB):
                valid = (t_idx <= s) if reverse else (t_idx >= s)
                e = jnp.exp(jnp.where(valid, bi - b[a0 + s:a0 + s + 1], -jnp.inf))
                w = jnp.sum(qi * kk[a0 + s:a0 + s + 1] * e, axis=-1, keepdims=True)
                o = o + w * v32[a0 + s:a0 + s + 1]
            if reverse and i < nsub - 1:
                lo, hi, ref_row = a0 + SUB, r0 + HCHUNK, a0 + SUB
            elif (not reverse) and i > 0:
                lo, hi, ref_row = r0, a0, a0 - 1
            else:
                lo = hi = ref_row = None
            if lo is not None:
                b_ref = b[ref_row:ref_row + 1]
                q_sc = (qi * jnp.exp(bi - b_ref)).astype(BF16)
                k_sc = (kk[lo:hi] * jnp.exp(b_ref - b[lo:hi])).astype(BF16)
                sc = lax.dot_general(q_sc, k_sc, (((1,), (1,)), ((), ())), preferred_element_type=F32)
                o = o + jnp.dot(sc.astype(BF16), v_bf[lo:hi], preferred_element_type=F32)
            subs.append(o)
        outs.append(jnp.concatenate(subs, axis=0))
    return jnp.concatenate(outs, axis=0)


def _hgrn_out_kernel(q_ref, ff_ref, fb_ref, v_ref, gt_ref, sf_ref, sb_ref, lb_ref, ng_ref, y_ref, *, nchunk):
    qs = _silu(q_ref[...].astype(F32))
    v_bf = v_ref[...]
    o = jnp.zeros((TILE, HGRN_K), F32)
    for reverse, f_ref, st_ref in ((False, ff_ref, sf_ref), (True, fb_ref, sb_ref)):
        lb = lb_ref[0, 1:2, :] if reverse else lb_ref[0, 0:1, :]
        kk, b = _hgrn_gates(f_ref[...], lb, reverse)
        o = o + _hgrn_direction(qs, kk, b, v_bf, st_ref, reverse, nchunk)
    o = o * lax.rsqrt(jnp.mean(o * o, axis=-1, keepdims=True) + EPS) * ng_ref[0]
    y_ref[...] = (o * _silu(gt_ref[...].astype(F32))).astype(BF16)


def _hgrn_out(z, sf, sb, lbt, ngt, batch, s_all):
    n = z.shape[0]
    nt = s_all // TILE
    nchunk = TILE // HCHUNK
    cols = [c // HGRN_K for c in (COL_B_Q, COL_B_FF, COL_B_FB, COL_B_I, COL_B_G)]

    def zspec(c):
        return pl.BlockSpec((TILE, HGRN_K), lambda b, h, i: (b * nt + i, c + h))

    st_spec = pl.BlockSpec((1, 1, nchunk, HGRN_K, HGRN_K), lambda b, h, i: (b, h, i, 0, 0))
    kern = functools.partial(_hgrn_out_kernel, nchunk=nchunk)
    return pl.pallas_call(
        kern,
        out_shape=jax.ShapeDtypeStruct((n, HGRN_DIM), BF16),
        grid=(batch, HGRN_HEADS, nt),
        in_specs=[zspec(c) for c in cols] + [
            st_spec, st_spec,
            pl.BlockSpec((1, 2, HGRN_K), lambda b, h, i: (h, 0, 0)),
            pl.BlockSpec((1, 1, HGRN_K), lambda b, h, i: (h, 0, 0)),
        ],
        out_specs=pl.BlockSpec((TILE, HGRN_K), lambda b, h, i: (b * nt + i, h)),
        compiler_params=_cparams(("arbitrary",) * 3),
        name="hgrn_out",
    )(z, z, z, z, z, sf, sb, lbt, ngt)


def _route(logits_t, rb):
    scores = jax.nn.sigmoid(logits_t)
    sel = scores + rb
    t = logits_t.shape[1]
    rows = [sel[e:e + 1] for e in range(N_EXPERTS)]
    sc_rows = [scores[e:e + 1] for e in range(N_EXPERTS)]

    def group_score(gr):
        best = None
        for a in range(EXPERTS_PER_GROUP):
            for c in range(a + 1, EXPERTS_PER_GROUP):
                pair = gr[a] + gr[c]
                best = pair if best is None else jnp.maximum(best, pair)
        return best

    gs = [group_score(rows[g * 4:(g + 1) * 4]) for g in range(N_GROUPS)]
    best_g = jnp.zeros((1, t), jnp.int32)
    best_v = gs[0]
    for g in range(1, N_GROUPS):
        better = gs[g] > best_v
        best_g = jnp.where(better, g, best_g)
        best_v = jnp.where(better, gs[g], best_v)
    cand, craw = [], []
    for a in range(EXPERTS_PER_GROUP):
        cv, cr = rows[a], sc_rows[a]
        for g in range(1, N_GROUPS):
            cv = jnp.where(best_g == g, rows[g * 4 + a], cv)
            cr = jnp.where(best_g == g, sc_rows[g * 4 + a], cr)
        cand.append(cv)
        craw.append(cr)

    def argmax_first(vals):
        bi = jnp.zeros((1, t), jnp.int32)
        bv = vals[0]
        for a in range(1, EXPERTS_PER_GROUP):
            better = vals[a] > bv
            bi = jnp.where(better, a, bi)
            bv = jnp.where(better, vals[a], bv)
        return bi

    i1 = argmax_first(cand)
    i2 = argmax_first([jnp.where(i1 == a, -jnp.inf, cand[a]) for a in range(EXPERTS_PER_GROUP)])

    def pick(vals, idx):
        out = vals[0]
        for a in range(1, EXPERTS_PER_GROUP):
            out = jnp.where(idx == a, vals[a], out)
        return out

    w1, w2 = pick(craw, i1), pick(craw, i2)
    tot = w1 + w2
    return (best_g * 4 + i1, best_g * 4 + i2), (w1 / tot, w2 / tot)


def _merge_kernel(h_ref, ga_ref, gb_ref, gc_ref, av_ref, ab_ref, ac_ref,
                  avp_ref, acp_ref, avn_ref, acn_ref, yb_ref, yc_ref,
                  cw_ref, wa_ref, wb_ref, wc_ref, wo_ref, mod_ref, g2_ref, rw_ref, rb_ref,
                  h1_ref, m2_ref, ids_ref, wts_ref, *, nt):
    i = pl.program_id(0)
    ti = i % nt
    row = lax.broadcasted_iota(jnp.int32, (TILE, 1), 0)
    has_prev = ti > 1
    has_next = jnp.logical_and(ti > 0, ti < nt - 1)

    v = ac_ref[...].astype(F32) * av_ref[...].astype(F32)
    vp = acp_ref[15:16, :].astype(F32) * avp_ref[15:16, :].astype(F32)
    vn = acn_ref[0:1, :].astype(F32) * avn_ref[0:1, :].astype(F32)
    vp = jnp.where(has_prev, vp, 0.0)
    vn = jnp.where(has_next, vn, 0.0)
    v_prev = jnp.where(row == 0, vp, pltpu.roll(v, 1, axis=0))
    v_next = jnp.where(row == TILE - 1, vn, pltpu.roll(v, TILE - 1, axis=0))
    conv = v_prev * cw_ref[0:1, :] + v * cw_ref[1:2, :] + v_next * cw_ref[2:3, :]
    xa = (ab_ref[...].astype(F32) * conv).astype(BF16)

    ya = jnp.dot(xa, wa_ref[...], preferred_element_type=F32)
    yb = jnp.dot(yb_ref[...], wb_ref[...], preferred_element_type=F32)
    yc = lax.dot_general(yc_ref[...], wc_ref[...], (((0,), (0,)), ((), ())), preferred_element_type=F32)
    m = (jax.nn.sigmoid(ga_ref[...].astype(F32)) * ya + jax.nn.sigmoid(gb_ref[...].astype(F32)) * yb
         + jax.nn.sigmoid(gc_ref[...].astype(F32)) * yc)
    out = jnp.dot(m.astype(BF16), wo_ref[...], preferred_element_type=F32)
    h1 = h_ref[...] + mod_ref[0, 0, 2:3, :] * out
    h1_ref[...] = h1

    y = h1 * lax.rsqrt(jnp.mean(h1 * h1, axis=-1, keepdims=True) + EPS) * g2_ref[...]
    m2 = y * (1.0 + mod_ref[0, 0, 4:5, :]) + mod_ref[0, 0, 3:4, :]
    m2_ref[...] = m2
    logits_t = lax.dot_general(rw_ref[...], m2, (((1,), (1,)), ((), ())),
                               precision=lax.Precision.HIGHEST, preferred_element_type=F32)
    ids, wts = _route(logits_t, rb_ref[...])
    for k in range(2):
        ids_ref[k:k + 1, :] = ids[k]
        wts_ref[k:k + 1, :] = wts[k]


def _merge(h, z, yb, yc, conv_w, wa, wb, wc, wo, modtab, g2, rw_t, rb, s_all):
    n = h.shape[0]
    nt = s_all // TILE
    hb = TILE // 16
    last16 = n // 16 - 1

    def zspec(width, col):
        return pl.BlockSpec((TILE, width), lambda i: (i, col // width))

    def halo(col, nxt):
        if nxt:
            return pl.BlockSpec((16, CONV_DIM), lambda i: (jnp.minimum((i + 1) * hb, last16), col // CONV_DIM))
        return pl.BlockSpec((16, CONV_DIM), lambda i: (jnp.maximum(i * hb - 1, 0), col // CONV_DIM))

    def full(shape):
        return pl.BlockSpec(shape, lambda i: (0,) * len(shape))

    kern = functools.partial(_merge_kernel, nt=nt)
    return pl.pallas_call(
        kern,
        out_shape=(
            jax.ShapeDtypeStruct((n, D_MODEL), F32),
            jax.ShapeDtypeStruct((n, D_MODEL), F32),
            jax.ShapeDtypeStruct((2, n), jnp.int32),
            jax.ShapeDtypeStruct((2, n), F32),
        ),
        grid=(n // TILE,),
        in_specs=[
            pl.BlockSpec((TILE, D_MODEL), lambda i: (i, 0)),
            zspec(D_MODEL, COL_GATE), zspec(D_MODEL, COL_GATE + D_MODEL), zspec(D_MODEL, COL_GATE + 2 * D_MODEL),
            zspec(CONV_DIM, COL_A_VAL), zspec(CONV_DIM, COL_A_B), zspec(CONV_DIM, COL_A_C),
            halo(COL_A_VAL, False), halo(COL_A_C, False), halo(COL_A_VAL, True), halo(COL_A_C, True),
            pl.BlockSpec((TILE, HGRN_DIM), lambda i: (i, 0)),
            pl.BlockSpec((ATT_DIM, TILE), lambda i: (0, i)),
            full((3, CONV_DIM)),
            full((CONV_DIM, D_MODEL)), full((HGRN_DIM, D_MODEL)), full((ATT_DIM, D_MODEL)),
            full((D_MODEL, D_MODEL)),
            pl.BlockSpec((1, 1, 6, D_MODEL), lambda i: (i // nt, jnp.minimum(i % nt, 1), 0, 0)),
            full((1, D_MODEL)),
            full((N_EXPERTS, D_MODEL)),
            full((N_EXPERTS, 1)),
        ],
        out_specs=(
            pl.BlockSpec((TILE, D_MODEL), lambda i: (i, 0)),
            pl.BlockSpec((TILE, D_MODEL), lambda i: (i, 0)),
            pl.BlockSpec((2, TILE), lambda i: (0, i)),
            pl.BlockSpec((2, TILE), lambda i: (0, i)),
        ),
        compiler_params=_cparams(("arbitrary",)),
        name="merge_route",
    )(h, z, z, z, z, z, z, z, z, z, z, yb, yc, conv_w, wa, wb, wc, wo, modtab, g2, rw_t, rb)


def _dispatch_kernel(dest_ref, m2_ref, xs_in, xs_hbm, dsm, sem_s, sem):
    del xs_in
    cp = pltpu.make_async_copy(dest_ref.at[0, 0], dsm, sem_s)
    cp.start()
    cp.wait()

    def row_copy(t, k):
        return pltpu.make_async_copy(m2_ref.at[pl.ds(t, 1)],
                                     xs_hbm.at[pl.ds(dsm[k * TILE + t], 1)], sem)

    def issue(t, carry):
        row_copy(t, 0).start()
        row_copy(t, 1).start()
        return carry

    lax.fori_loop(0, TILE, issue, 0, unroll=DMA_UNROLL)

    for _ in range(2):
        pltpu.make_async_copy(m2_ref, xs_hbm.at[pl.ds(0, TILE)], sem).wait()


def _dispatch(dest_tiles, m2, xs_init):
    n = m2.shape[0]
    return pl.pallas_call(
        _dispatch_kernel,
        out_shape=jax.ShapeDtypeStruct(xs_init.shape, F32),
        grid=(n // TILE,),
        in_specs=[
            pl.BlockSpec((1, 1, 2 * TILE), lambda i: (i, 0, 0)),
            pl.BlockSpec((TILE, D_MODEL), lambda i: (i, 0)),
            pl.BlockSpec(memory_space=pl.ANY),
        ],
        out_specs=pl.BlockSpec(memory_space=pl.ANY),
        scratch_shapes=[pltpu.SMEM((2 * TILE,), jnp.int32), pltpu.SemaphoreType.DMA, pltpu.SemaphoreType.DMA],
        input_output_aliases={2: 0},
        compiler_params=_cparams(("arbitrary",)),
        name="moe_dispatch",
    )(dest_tiles, m2, xs_init)


def _ffn_kernel(be_ref, bs_ref, nv_ref, x_ref, wg_ref, wu_ref, wd_ref, y_ref):
    del be_ref, bs_ref
    valid = pl.program_id(0) < nv_ref[0]

    @pl.when(valid)
    def _():
        x = x_ref[...].astype(BF16)
        hg = jnp.dot(x, wg_ref[0], preferred_element_type=F32)
        hu = jnp.dot(x, wu_ref[0], preferred_element_type=F32)
        hid = (_silu(hg) * hu).astype(BF16)
        y_ref[...] = jnp.dot(hid, wd_ref[0], preferred_element_type=F32)

    @pl.when(jnp.logical_not(valid))
    def _():
        y_ref[...] = jnp.zeros(y_ref.shape, F32)


def _expert_ffn(blk_e, blk_src, nvalid, xs, wg, wu, wd):
    p = xs.shape[0]
    nb = p // MOE_BM
    grid_spec = pltpu.PrefetchScalarGridSpec(
        num_scalar_prefetch=3,
        grid=(nb,),
        in_specs=[
            pl.BlockSpec((MOE_BM, D_MODEL), lambda i, be, bs, nv: (bs[i], 0)),
            pl.BlockSpec((1, D_MODEL, EXPERT_DFF), lambda i, be, bs, nv: (be[i], 0, 0)),
            pl.BlockSpec((1, D_MODEL, EXPERT_DFF), lambda i, be, bs, nv: (be[i], 0, 0)),
            pl.BlockSpec((1, EXPERT_DFF, D_MODEL), lambda i, be, bs, nv: (be[i], 0, 0)),
        ],
        out_specs=pl.BlockSpec((MOE_BM, D_MODEL), lambda i, be, bs, nv: (i, 0)),
    )
    return pl.pallas_call(
        _ffn_kernel,
        out_shape=jax.ShapeDtypeStruct((p, D_MODEL), F32),
        grid_spec=grid_spec,
        compiler_params=_cparams(("arbitrary",)),
        name="expert_ffn",
    )(blk_e, blk_src, nvalid, xs, wg, wu, wd)


def _combine_kernel(dest_ref, h_ref, w_ref, mod_ref, fg_ref, y_hbm, o_ref, dsm, ybuf, sem_s, sem, *, final):
    cp = pltpu.make_async_copy(dest_ref.at[0, 0], dsm, sem_s)
    cp.start()
    cp.wait()

    def row_copy(t, k):
        return pltpu.make_async_copy(y_hbm.at[pl.ds(dsm[k * TILE + t], 1)],
                                     ybuf.at[k, pl.ds(t, 1)], sem)

    def issue(t, carry):
        row_copy(t, 0).start()
        row_copy(t, 1).start()
        return carry

    lax.fori_loop(0, TILE, issue, 0, unroll=DMA_UNROLL)

    for k in range(2):
        pltpu.make_async_copy(y_hbm.at[pl.ds(0, TILE)], ybuf.at[k], sem).wait()

    w = w_ref[...]
    moe = w[:, 0:1] * ybuf[0] + w[:, 1:2] * ybuf[1]
    h2 = h_ref[...] + mod_ref[0, 0, 5:6, :] * moe
    if final:
        h2 = h2 * lax.rsqrt(jnp.mean(h2 * h2, axis=-1, keepdims=True) + EPS) * fg_ref[...]
    o_ref[...] = h2


def _combine(dest_tiles, h1, wts_col, modtab, final_g, y, batch, s_all, final):
    nt = s_all // TILE
    skip = 1 if final else 0
    nto = nt - skip

    def src(i):
        return (i // nto) * nt + i % nto + skip

    kern = functools.partial(_combine_kernel, final=final)
    return pl.pallas_call(
        kern,
        out_shape=jax.ShapeDtypeStruct((batch * nto * TILE, D_MODEL), F32),
        grid=(batch * nto,),
        in_specs=[
            pl.BlockSpec((1, 1, 2 * TILE), lambda i: (src(i), 0, 0)),
            pl.BlockSpec((TILE, D_MODEL), lambda i: (src(i), 0)),
            pl.BlockSpec((TILE, 2), lambda i: (src(i), 0)),
            pl.BlockSpec((1, 1, 6, D_MODEL), lambda i: (i // nto, jnp.minimum(i % nto + skip, 1), 0, 0)),
            pl.BlockSpec((1, D_MODEL), lambda i: (0, 0)),
            pl.BlockSpec(memory_space=pl.ANY),
        ],
        out_specs=pl.BlockSpec((TILE, D_MODEL), lambda i: (i, 0)),
        scratch_shapes=[
            pltpu.SMEM((2 * TILE,), jnp.int32),
            pltpu.VMEM((2, TILE, D_MODEL), F32),
            pltpu.SemaphoreType.DMA,
            pltpu.SemaphoreType.DMA,
        ],
        compiler_params=_cparams(("arbitrary",)),
        name="moe_combine",
    )(dest_tiles, h1, wts_col, modtab, final_g, y)


def _dispatch_plan(ids, n):
    flat_e = ids.reshape(-1)
    onehot = (flat_e[:, None] == jnp.arange(N_EXPERTS, dtype=jnp.int32)[None, :]).astype(jnp.int32)
    csum = jnp.cumsum(onehot, axis=0)
    counts = csum[-1]
    padded = (counts + MOE_BM - 1) // MOE_BM * MOE_BM
    pend = jnp.cumsum(padded)
    pstart = pend - padded
    dest = jnp.sum(onehot * (pstart[None, :] + csum - 1), axis=1).astype(jnp.int32)
    nb = -(-(2 * n) // MOE_BM) + N_EXPERTS
    nvalid = (pend[-1] // MOE_BM).astype(jnp.int32)
    blk = jnp.arange(nb, dtype=jnp.int32)
    blk_src = jnp.minimum(blk, nvalid - 1)
    blk_e = jnp.minimum(jnp.sum((blk_src[:, None] * MOE_BM >= pend[None, :]).astype(jnp.int32), axis=1),
                        N_EXPERTS - 1).astype(jnp.int32)
    dest2 = dest.reshape(2, n // TILE, TILE)
    dest_tiles = jnp.transpose(dest2, (1, 0, 2)).reshape(n // TILE, 1, 2 * TILE)
    return dest_tiles, blk_e, blk_src, nvalid.reshape(1), nb


def _reorder_w_in(w):
    rest = w[:, 0:4864]
    gates = w[:, 4864:W_IN_USED]
    pad = jnp.zeros((w.shape[0], W_IN_COLS - W_IN_USED), w.dtype)
    return jnp.concatenate([gates, rest, pad], axis=1).astype(BF16)


def _rope_tables(rows, ctx):
    row = jnp.repeat(jnp.arange(rows), GRID_W).astype(F32)
    col = (jnp.arange(rows * GRID_W) % GRID_W).astype(F32)
    inv = ROPE_THETA ** (-jnp.arange(0, ROPE_AXIS_DIM, 2, dtype=F32) / ROPE_AXIS_DIM)
    ar = row[:, None] * inv
    ac = col[:, None] * inv
    ang = jnp.concatenate([ar, ar, ac, ac], axis=-1)
    cos = jnp.concatenate([jnp.ones((ctx, HEAD_DIM), F32), jnp.cos(ang)], axis=0)
    sin = jnp.concatenate([jnp.zeros((ctx, HEAD_DIM), F32), jnp.sin(ang)], axis=0)
    first = (jnp.arange(HEAD_DIM) % ROPE_AXIS_DIM) < (ROPE_AXIS_DIM // 2)
    sa = jnp.where(first[None, :], -sin, 0.0)
    sb = jnp.where(first[None, :], 0.0, sin)
    rep = ATT_DIM // HEAD_DIM
    return jnp.tile(cos, (1, rep)), jnp.tile(sa, (1, rep)), jnp.tile(sb, (1, rep))


def kernel(x, c, ctx, c_ctx, w_mod, b_mod, norm1_g, norm2_g, w_in, conv_w, lb_param, hgrn_norm_g,
           q_norm_g, k_norm_g, w_a_out, w_b_out, w_c_out, w_o, router_w, router_b, w_gate, w_up,
           w_down, final_g):
    batch, seq, _ = x.shape
    n_ctx = ctx.shape[1]
    depth = w_mod.shape[0]
    assert n_ctx == TILE and seq % TILE == 0 and seq % GRID_W == 0 and batch + 1 <= 8
    s_all = n_ctx + seq
    n = batch * s_all

    h = jnp.concatenate([ctx, x], axis=1).reshape(n, D_MODEL)

    cs = jnp.zeros((8, D_MODEL), F32).at[:batch].set(c).at[batch].set(c_ctx)
    mods = _modulation(cs, w_mod, b_mod)
    lat = mods[:, :batch].reshape(depth, batch, 6, D_MODEL)
    cxt = jnp.broadcast_to(mods[:, batch].reshape(depth, 1, 6, D_MODEL), lat.shape)
    modtabs = jnp.stack([cxt, lat], axis=2)

    p = jax.nn.softmax(lb_param.astype(F32), axis=0)
    cum = jnp.cumsum(p, axis=0)
    lower = cum - cum[0:1]

    cos, sa, sb = _rope_tables(seq // GRID_W, n_ctx)
    seg = jnp.arange(ATT_DIM) // HEAD_DIM
    bd = (seg[:, None] == seg[None, :]).astype(BF16)
    rw_t = router_w.T.astype(F32)
    rb = router_b.reshape(N_EXPERTS, 1).astype(F32)
    fg = final_g.reshape(1, D_MODEL)

    out = None
    for layer in range(depth):
        modtab = modtabs[layer]
        z = _in_proj(h, modtab, norm1_g[layer].reshape(1, D_MODEL), _reorder_w_in(w_in[layer]), s_all, n_ctx)

        lbt = jnp.transpose(lower[layer].reshape(2, HGRN_HEADS, HGRN_K), (1, 0, 2))
        ngt = hgrn_norm_g[layer].reshape(HGRN_HEADS, 1, HGRN_K)
        sf, sbk = _hgrn_states(z, lbt, batch, s_all)
        yb = _hgrn_out(z, sf, sbk, lbt, ngt, batch, s_all)

        qg = jnp.tile(q_norm_g[layer], ATT_HEADS).reshape(1, ATT_DIM)
        kg = jnp.tile(k_norm_g[layer], ATT_HEADS).reshape(1, ATT_DIM)
        q_t, k_hm, v_t = _qk_prep(z, cos, sa, sb, qg, kg, bd, s_all)
        score_bound = (HEAD_DIM * HEAD_DIM ** -0.5 * LOG2E * ATT_BOUND_MARGIN
                       * jnp.max(jnp.abs(q_norm_g[layer])) * jnp.max(jnp.abs(k_norm_g[layer])))
        yc = lax.cond(score_bound <= ATT_SAFE_LOG2,
                      lambda a, b, c_: _attention(a, b, c_, batch, s_all, n_ctx, False),
                      lambda a, b, c_: _attention(a, b, c_, batch, s_all, n_ctx, True),
                      q_t, k_hm, v_t)

        h1, m2, ids, wts = _merge(
            h, z, yb, yc, conv_w[layer], w_a_out[layer].astype(BF16), w_b_out[layer].astype(BF16),
            w_c_out[layer].astype(BF16), w_o[layer].astype(BF16), modtab,
            norm2_g[layer].reshape(1, D_MODEL), rw_t, rb, s_all)

        dest_tiles, blk_e, blk_src, nvalid, nb = _dispatch_plan(ids, n)
        xs = _dispatch(dest_tiles, m2, jnp.zeros((nb * MOE_BM, D_MODEL), F32))
        y = _expert_ffn(blk_e, blk_src, nvalid, xs, w_gate[layer].astype(BF16), w_up[layer].astype(BF16),
                        w_down[layer].astype(BF16))
        wts_col = wts.T
        last = layer == depth - 1
        res = _combine(dest_tiles, h1, wts_col, modtab, fg, y, batch, s_all, last)
        if last:
            out = res.reshape(batch, seq, D_MODEL)
        else:
            h = res
    return out
```

```python
import functools

import jax
import jax.numpy as jnp
from jax import lax
from jax.experimental import pallas as pl
from jax.experimental.pallas import tpu as pltpu

F32 = jnp.float32
BF16 = jnp.bfloat16

D_MODEL = 1024
EPS = 1e-6
GRID_W = 64
CONV_DIM = 512
HGRN_DIM = 512
HGRN_HEADS = 4
HGRN_K = 128
SUB = 16
HCHUNK = 64
ATT_HEADS = 8
KV_HEADS = 2
ATT_GROUP = 4
HEAD_DIM = 64
ATT_DIM = 512
ROPE_THETA = 10000.0
ROPE_AXIS_DIM = 32
N_EXPERTS = 16
N_GROUPS = 4
EXPERTS_PER_GROUP = 4
EXPERT_DFF = 1024
MOE_BM = 256
DMA_UNROLL = 8

TILE = 256
VMEM_LIMIT = 48 * 1024 * 1024

COL_GATE = 0
COL_A_VAL, COL_A_B, COL_A_C = 3072, 3584, 4096
COL_B_Q, COL_B_FF, COL_B_FB, COL_B_I, COL_B_G = 4608, 5120, 5632, 6144, 6656
COL_C_Q, COL_C_K, COL_C_V = 7168, 7680, 7808
W_IN_USED = 7936
W_IN_COLS = 8192
IN_TN = 512
KV_DIM = KV_HEADS * HEAD_DIM
VT_ROWS = HEAD_DIM + 16
ATT_SAFE_LOG2 = 60.0
ATT_BOUND_MARGIN = 1.02
LOG2E = 1.4426950408889634


def _cparams(sem):
    return pltpu.CompilerParams(dimension_semantics=sem, vmem_limit_bytes=VMEM_LIMIT)


def _silu(x):
    return x * jax.nn.sigmoid(x)


def _mod_kernel(c_ref, w_ref, b_ref, o_ref):
    a = _silu(c_ref[...])
    o_ref[0] = jnp.dot(a, w_ref[0], precision=lax.Precision.HIGHEST,
                       preferred_element_type=F32) + b_ref[0]


def _modulation(cs, w_mod, b_mod):
    depth = w_mod.shape[0]
    ncol = w_mod.shape[2]
    tn = 1024
    return pl.pallas_call(
        _mod_kernel,
        out_shape=jax.ShapeDtypeStruct((depth, 8, ncol), F32),
        grid=(depth, ncol // tn),
        in_specs=[
            pl.BlockSpec((8, D_MODEL), lambda l, j: (0, 0)),
            pl.BlockSpec((1, D_MODEL, tn), lambda l, j: (l, 0, j)),
            pl.BlockSpec((1, 1, tn), lambda l, j: (l, 0, j)),
        ],
        out_specs=pl.BlockSpec((1, 8, tn), lambda l, j: (l, 0, j)),
        compiler_params=_cparams(("arbitrary", "arbitrary")),
        name="modulation",
    )(cs, w_mod, b_mod.reshape(depth, 1, ncol))


def _in_proj_kernel(h_ref, mod_ref, g_ref, w_ref, z_ref, u_ref, *, tm, s_all, ctx):
    i = pl.program_id(0)
    j = pl.program_id(1)

    @pl.when(j == 0)
    def _():
        x = h_ref[...]
        y = x * lax.rsqrt(jnp.mean(x * x, axis=-1, keepdims=True) + EPS) * g_ref[...]
        pos = (i * tm) % s_all + lax.broadcasted_iota(jnp.int32, (tm, 1), 0)
        is_ctx = pos < ctx
        sh = jnp.where(is_ctx, mod_ref[0, 0, 0:1, :], mod_ref[0, 1, 0:1, :])
        sc = jnp.where(is_ctx, mod_ref[0, 0, 1:2, :], mod_ref[0, 1, 1:2, :])
        u_ref[...] = (y * (1.0 + sc) + sh).astype(BF16)

    z_ref[...] = jnp.dot(u_ref[...], w_ref[...], preferred_element_type=F32).astype(BF16)


def _in_proj(h, modtab, g, w_in, s_all, ctx):
    n = h.shape[0]
    tm = next(t for t in (1280, 768, 512, 256) if s_all % t == 0)
    kern = functools.partial(_in_proj_kernel, tm=tm, s_all=s_all, ctx=ctx)
    return pl.pallas_call(
        kern,
        out_shape=jax.ShapeDtypeStruct((n, W_IN_COLS), BF16),
        grid=(n // tm, W_IN_COLS // IN_TN),
        in_specs=[
            pl.BlockSpec((tm, D_MODEL), lambda i, j: (i, 0)),
            pl.BlockSpec((1, 2, 6, D_MODEL), lambda i, j: ((i * tm) // s_all, 0, 0, 0)),
            pl.BlockSpec((1, D_MODEL), lambda i, j: (0, 0)),
            pl.BlockSpec((D_MODEL, IN_TN), lambda i, j: (0, j)),
        ],
        out_specs=pl.BlockSpec((tm, IN_TN), lambda i, j: (i, j)),
        scratch_shapes=[pltpu.VMEM((tm, D_MODEL), BF16)],
        compiler_params=_cparams(("arbitrary", "arbitrary")),
        name="in_proj",
    )(h, modtab, g, w_in)


def _seg_mean_sq(x, bd):
    sq = x * x
    hi = sq.astype(BF16)
    lo = (sq - hi.astype(F32)).astype(BF16)
    s = jnp.dot(hi, bd, preferred_element_type=F32) + jnp.dot(lo, bd, preferred_element_type=F32)
    return s * (1.0 / HEAD_DIM)


def _qk_kernel(q_ref, k_ref, v_ref, cos_ref, sa_ref, sb_ref, qg_ref, kg_ref, bd_ref,
               qt_ref, ko_ref, vt_ref):
    bd = bd_ref[...]

    def norm_rope(x, g, width):
        y = x * lax.rsqrt(_seg_mean_sq(x, bd[0:width, 0:width]) + EPS) * g
        up = pltpu.roll(y, width - 16, axis=1)
        dn = pltpu.roll(y, 16, axis=1)
        return y * cos_ref[:, 0:width] + up * sa_ref[:, 0:width] + dn * sb_ref[:, 0:width]

    q = norm_rope(q_ref[...].astype(F32), qg_ref[...], ATT_DIM)
    qt_ref[...] = jnp.transpose(q * (HEAD_DIM ** -0.5 * LOG2E)).astype(BF16)
    k = norm_rope(k_ref[...].astype(F32), kg_ref[:, 0:KV_DIM], KV_DIM)
    for hd in range(KV_HEADS):
        ko_ref[hd] = k[:, hd * HEAD_DIM:(hd + 1) * HEAD_DIM].astype(BF16)
    vt = jnp.transpose(v_ref[...].astype(F32)).astype(BF16)
    for hd in range(KV_HEADS):
        vt_ref[hd, 0:HEAD_DIM, :] = vt[hd * HEAD_DIM:(hd + 1) * HEAD_DIM, :]
        vt_ref[hd, HEAD_DIM:VT_ROWS, :] = jnp.ones((VT_ROWS - HEAD_DIM, TILE), BF16)


def _qk_prep(z, cos, sa, sb, qg, kg, bd, s_all):
    n = z.shape[0]
    nt = s_all // TILE
    tab = pl.BlockSpec((TILE, ATT_DIM), lambda i: (i % nt, 0))
    vec = pl.BlockSpec((1, ATT_DIM), lambda i: (0, 0))
    return pl.pallas_call(
        _qk_kernel,
        out_shape=(
            jax.ShapeDtypeStruct((ATT_DIM, n), BF16),
            jax.ShapeDtypeStruct((KV_HEADS, n, HEAD_DIM), BF16),
            jax.ShapeDtypeStruct((KV_HEADS, VT_ROWS, n), BF16),
        ),
        grid=(n // TILE,),
        in_specs=[
            pl.BlockSpec((TILE, ATT_DIM), lambda i: (i, COL_C_Q // ATT_DIM)),
            pl.BlockSpec((TILE, KV_DIM), lambda i: (i, COL_C_K // KV_DIM)),
            pl.BlockSpec((TILE, KV_DIM), lambda i: (i, COL_C_V // KV_DIM)),
            tab, tab, tab, vec, vec,
            pl.BlockSpec((ATT_DIM, ATT_DIM), lambda i: (0, 0)),
        ],
        out_specs=(
            pl.BlockSpec((ATT_DIM, TILE), lambda i: (0, i)),
            pl.BlockSpec((KV_HEADS, TILE, HEAD_DIM), lambda i: (0, i, 0)),
            pl.BlockSpec((KV_HEADS, VT_ROWS, TILE), lambda i: (0, 0, i)),
        ),
        compiler_params=_cparams(("arbitrary",)),
        name="qk_prep",
    )(z, z, z, cos, sa, sb, qg, kg, bd)


def _attn_kernel(qt_ref, k_ref, vt_ref, o_ref, qs_ref, m_ref, acc_ref, *, tq, tk, nk, ctx, online):
    qi = pl.program_id(2)
    ki = pl.program_id(3)
    is_ctx = qi == 0

    @pl.when(ki == 0)
    def _():
        for g in range(ATT_GROUP):
            qs_ref[:, g * tq:(g + 1) * tq] = qt_ref[g * HEAD_DIM:(g + 1) * HEAD_DIM, :]
        m_ref[...] = jnp.full(m_ref.shape, -jnp.inf, F32)
        acc_ref[...] = jnp.zeros(acc_ref.shape, F32)

    def scores(j):
        k = k_ref[0, j * ctx:(j + 1) * ctx, :]
        return jnp.dot(k, qs_ref[...], preferred_element_type=F32)

    def update(j, s):
        vt = vt_ref[0, :, j * ctx:(j + 1) * ctx]
        if online:
            m_old = m_ref[...]
            m_new = jnp.maximum(m_old, jnp.max(s, axis=0, keepdims=True))
            p = jnp.exp2(s - m_new)
            alpha = jnp.exp2(m_old - m_new)
            m_ref[...] = m_new
            pv = jnp.dot(vt, p.astype(BF16), preferred_element_type=F32)
            acc_ref[...] = acc_ref[...] * alpha + pv
        else:
            pv = jnp.dot(vt, jnp.exp2(s).astype(BF16), preferred_element_type=F32)
            acc_ref[...] = acc_ref[...] + pv

    @pl.when(jnp.logical_and(is_ctx, ki == 0))
    def _():
        update(0, scores(0))

    @pl.when(jnp.logical_not(is_ctx))
    def _():
        nblk = tk // ctx
        ahead = 2
        pending = [scores(j) for j in range(min(ahead, nblk))]
        for j in range(nblk):
            if j + ahead < nblk:
                pending.append(scores(j + ahead))
            update(j, pending.pop(0))

    last = jnp.where(is_ctx, 0, nk - 1)

    @pl.when(ki == last)
    def _():
        out = acc_ref[0:HEAD_DIM, :] / acc_ref[HEAD_DIM:HEAD_DIM + 1, :]
        for g in range(ATT_GROUP):
            o_ref[g * HEAD_DIM:(g + 1) * HEAD_DIM, :] = out[:, g * tq:(g + 1) * tq].astype(BF16)


def _attention(q_t, k_hm, v_t, batch, s_all, ctx, online):
    n = q_t.shape[1]
    tq = TILE
    tk = next(t for t in (3328, 1280, 768, 512, 256) if s_all % t == 0)
    nq, nk = s_all // tq, s_all // tk
    grp = ATT_GROUP * HEAD_DIM

    def kv_idx(ki, qi):
        return jnp.where(qi == 0, 0, ki)

    kern = functools.partial(_attn_kernel, tq=tq, tk=tk, nk=nk, ctx=ctx, online=online)
    return pl.pallas_call(
        kern,
        out_shape=jax.ShapeDtypeStruct((ATT_DIM, n), BF16),
        grid=(batch, KV_HEADS, nq, nk),
        in_specs=[
            pl.BlockSpec((grp, tq), lambda b, h, qi, ki: (h, b * nq + qi)),
            pl.BlockSpec((1, tk, HEAD_DIM), lambda b, h, qi, ki: (h, b * nk + kv_idx(ki, qi), 0)),
            pl.BlockSpec((1, VT_ROWS, tk), lambda b, h, qi, ki: (h, 0, b * nk + kv_idx(ki, qi))),
        ],
        out_specs=pl.BlockSpec((grp, tq), lambda b, h, qi, ki: (h, b * nq + qi)),
        scratch_shapes=[
            pltpu.VMEM((HEAD_DIM, ATT_GROUP * tq), BF16),
            pltpu.VMEM((1, ATT_GROUP * tq), F32),
            pltpu.VMEM((VT_ROWS, ATT_GROUP * tq), F32),
        ],
        compiler_params=_cparams(("arbitrary",) * 4),
        name="attention",
    )(q_t, k_hm, v_t)


def _chunk_cumsum(g, reverse):
    rows = g.shape[0]
    pos = lax.broadcasted_iota(jnp.int32, (rows, 1), 0) % HCHUNK
    b = g
    s = 1
    while s < HCHUNK:
        if reverse:
            shifted = pltpu.roll(b, rows - s, axis=0)
            keep = pos < HCHUNK - s
        else:
            shifted = pltpu.roll(b, s, axis=0)
            keep = pos >= s
        b = b + jnp.where(keep, shifted, 0.0)
        s *= 2
    return b


def _hgrn_gates(f_raw, lb, reverse):
    fg = lb + (1.0 - lb) * jax.nn.sigmoid(f_raw.astype(F32))
    return 1.0 - fg, _chunk_cumsum(jnp.log(fg), reverse)


def _hgrn_state_kernel(ff_ref, fb_ref, vf_ref, vb_ref, lb_ref, sf_ref, sb_ref, stf_ref, stb_ref, *, nchunk):
    i = pl.program_id(2)

    @pl.when(i == 0)
    def _():
        stf_ref[...] = jnp.zeros(stf_ref.shape, F32)
        stb_ref[...] = jnp.zeros(stb_ref.shape, F32)

    for reverse, f_ref, v_ref, out_ref, st_ref in ((False, ff_ref, vf_ref, sf_ref, stf_ref),
                                                    (True, fb_ref, vb_ref, sb_ref, stb_ref)):
        lb = lb_ref[0, 1:2, :] if reverse else lb_ref[0, 0:1, :]
        kk, b = _hgrn_gates(f_ref[...], lb, reverse)
        v = v_ref[...]
        st = st_ref[...]
        order = range(nchunk - 1, -1, -1) if reverse else range(nchunk)
        for c in order:
            r0 = c * HCHUNK
            out_ref[0, 0, c] = st.astype(BF16)
            bc = b[r0:r0 + HCHUNK]
            b_last = bc[0:1] if reverse else bc[HCHUNK - 1:HCHUNK]
            kt = (kk[r0:r0 + HCHUNK] * jnp.exp(b_last - bc)).astype(BF16)
            ds = lax.dot_general(v[r0:r0 + HCHUNK], kt, (((0,), (0,)), ((), ())),
                                 preferred_element_type=F32)
            st = st * jnp.exp(b_last) + ds
        st_ref[...] = st


def _hgrn_states(z, lbt, batch, s_all):
    nt = s_all // TILE
    nchunk = TILE // HCHUNK
    cff, cfb, cv = COL_B_FF // HGRN_K, COL_B_FB // HGRN_K, COL_B_I // HGRN_K

    def rev(i):
        return jnp.where(i == 0, 0, nt - i)

    st_shape = jax.ShapeDtypeStruct((batch, HGRN_HEADS, nt * nchunk, HGRN_K, HGRN_K), BF16)
    kern = functools.partial(_hgrn_state_kernel, nchunk=nchunk)
    return pl.pallas_call(
        kern,
        out_shape=(st_shape, st_shape),
        grid=(batch, HGRN_HEADS, nt),
        in_specs=[
            pl.BlockSpec((TILE, HGRN_K), lambda b, h, i: (b * nt + i, cff + h)),
            pl.BlockSpec((TILE, HGRN_K), lambda b, h, i: (b * nt + rev(i), cfb + h)),
            pl.BlockSpec((TILE, HGRN_K), lambda b, h, i: (b * nt + i, cv + h)),
            pl.BlockSpec((TILE, HGRN_K), lambda b, h, i: (b * nt + rev(i), cv + h)),
            pl.BlockSpec((1, 2, HGRN_K), lambda b, h, i: (h, 0, 0)),
        ],
        out_specs=(
            pl.BlockSpec((1, 1, nchunk, HGRN_K, HGRN_K), lambda b, h, i: (b, h, i, 0, 0)),
            pl.BlockSpec((1, 1, nchunk, HGRN_K, HGRN_K), lambda b, h, i: (b, h, rev(i), 0, 0)),
        ),
        scratch_shapes=[pltpu.VMEM((HGRN_K, HGRN_K), F32), pltpu.VMEM((HGRN_K, HGRN_K), F32)],
        compiler_params=_cparams(("arbitrary",) * 3),
        name="hgrn_states",
    )(z, z, z, z, lbt)


def _hgrn_matmul_terms(qs, gates, v_bf, st_refs, nchunk):
    nsub = HCHUNK // SUB
    nt = (((1,), (1,)), ((), ()))
    inter, scores = {}, {}
    for d, (kk, b) in enumerate(gates):
        reverse = d == 1
        for c in range(nchunk):
            r0 = c * HCHUNK
            qt = (qs[r0:r0 + HCHUNK] * jnp.exp(b[r0:r0 + HCHUNK])).astype(BF16)
            inter[d, c] = lax.dot_general(qt, st_refs[d][0, 0, c], nt, preferred_element_type=F32)
            for i in range(nsub):
                a0 = r0 + i * SUB
                if reverse and i < nsub - 1:
                    lo, hi, ref_row = a0 + SUB, r0 + HCHUNK, a0 + SUB
                elif (not reverse) and i > 0:
                    lo, hi, ref_row = r0, a0, a0 - 1
                else:
                    continue
                b_ref = b[ref_row:ref_row + 1]
                q_sc = (qs[a0:a0 + SUB] * jnp.exp(b[a0:a0 + SUB] - b_ref)).astype(BF16)
                k_sc = (kk[lo:hi] * jnp.exp(b_ref - b[lo:hi])).astype(BF16)
                scores[d, c, i] = (lo, hi, lax.dot_general(q_sc, k_sc, nt, preferred_element_type=F32))
    total = None
    for d in range(len(gates)):
        outs = []
        for c in range(nchunk):
            subs = []
            for i in range(nsub):
                o = inter[d, c][i * SUB:(i + 1) * SUB]
                if (d, c, i) in scores:
                    lo, hi, sc = scores[d, c, i]
                    o = o + jnp.dot(sc.astype(BF16), v_bf[lo:hi], preferred_element_type=F32)
                subs.append(o)
            outs.append(jnp.concatenate(subs, axis=0))
        part = jnp.concatenate(outs, axis=0)
        total = part if total is None else total + part
    return total


def _hgrn_exact_terms(qs, kk, b, v32, reverse):
    t8 = lax.broadcasted_iota(jnp.int32, (8, 1), 0)
    groups = []
    for a0 in range(0, qs.shape[0], SUB):
        for h0 in range(0, SUB, 8):
            qh = qs[a0 + h0:a0 + h0 + 8]
            bh = b[a0 + h0:a0 + h0 + 8]
            oh = None
            for s in (range(h0, SUB) if reverse else range(h0 + 8)):
                diff = bh - b[a0 + s:a0 + s + 1]
                if h0 <= s < h0 + 8:
                    keep = (t8 + h0 <= s) if reverse else (t8 + h0 >= s)
                    diff = jnp.where(keep, diff, -jnp.inf)
                w = jnp.sum(qh * kk[a0 + s:a0 + s + 1] * jnp.exp(diff), axis=-1, keepdims=True)
                term = w * v32[a0 + s:a0 + s + 1]
                oh = term if oh is None else oh + term
            groups.append(oh)
    return jnp.concatenate(groups, axis=0)


def _hgrn_out_kernel(q_ref, ff_ref, fb_ref, v_ref, gt_ref, sf_ref, sb_ref, lb_ref, ng_ref, y_ref, *, nchunk):
    qs = _silu(q_ref[...].astype(F32))
    v_bf = v_ref[...]
    v32 = v_bf.astype(F32)
    gates = []
    for reverse, f_ref in ((False, ff_ref), (True, fb_ref)):
        lb = lb_ref[0, 1:2, :] if reverse else lb_ref[0, 0:1, :]
        gates.append(_hgrn_gates(f_ref[...], lb, reverse))
    o = _hgrn_matmul_terms(qs, gates, v_bf, (sf_ref, sb_ref), nchunk)
    o = o + _hgrn_exact_terms(qs, gates[0][0], gates[0][1], v32, False)
    o = o + _hgrn_exact_terms(qs, gates[1][0], gates[1][1], v32, True)
    o = o * lax.rsqrt(jnp.mean(o * o, axis=-1, keepdims=True) + EPS) * ng_ref[0]
    y_ref[...] = (o * _silu(gt_ref[...].astype(F32))).astype(BF16)


def _hgrn_out(z, sf, sb, lbt, ngt, batch, s_all):
    n = z.shape[0]
    nt = s_all // TILE
    nchunk = TILE // HCHUNK
    cols = [c // HGRN_K for c in (COL_B_Q, COL_B_FF, COL_B_FB, COL_B_I, COL_B_G)]

    def zspec(c):
        return pl.BlockSpec((TILE, HGRN_K), lambda b, h, i: (b * nt + i, c + h))

    st_spec = pl.BlockSpec((1, 1, nchunk, HGRN_K, HGRN_K), lambda b, h, i: (b, h, i, 0, 0))
    kern = functools.partial(_hgrn_out_kernel, nchunk=nchunk)
    return pl.pallas_call(
        kern,
        out_shape=jax.ShapeDtypeStruct((n, HGRN_DIM), BF16),
        grid=(batch, HGRN_HEADS, nt),
        in_specs=[zspec(c) for c in cols] + [
            st_spec, st_spec,
            pl.BlockSpec((1, 2, HGRN_K), lambda b, h, i: (h, 0, 0)),
            pl.BlockSpec((1, 1, HGRN_K), lambda b, h, i: (h, 0, 0)),
        ],
        out_specs=pl.BlockSpec((TILE, HGRN_K), lambda b, h, i: (b * nt + i, h)),
        compiler_params=_cparams(("arbitrary",) * 3),
        name="hgrn_out",
    )(z, z, z, z, z, sf, sb, lbt, ngt)


def _route(logits_t, rb):
    scores = jax.nn.sigmoid(logits_t)
    sel = scores + rb
    t = logits_t.shape[1]
    rows = [sel[e:e + 1] for e in range(N_EXPERTS)]
    sc_rows = [scores[e:e + 1] for e in range(N_EXPERTS)]

    def group_score(gr):
        best = None
        for a in range(EXPERTS_PER_GROUP):
            for c in range(a + 1, EXPERTS_PER_GROUP):
                pair = gr[a] + gr[c]
                best = pair if best is None else jnp.maximum(best, pair)
        return best

    gs = [group_score(rows[g * 4:(g + 1) * 4]) for g in range(N_GROUPS)]
    best_g = jnp.zeros((1, t), jnp.int32)
    best_v = gs[0]
    for g in range(1, N_GROUPS):
        better = gs[g] > best_v
        best_g = jnp.where(better, g, best_g)
        best_v = jnp.where(better, gs[g], best_v)
    cand, craw = [], []
    for a in range(EXPERTS_PER_GROUP):
        cv, cr = rows[a], sc_rows[a]
        for g in range(1, N_GROUPS):
            cv = jnp.where(best_g == g, rows[g * 4 + a], cv)
            cr = jnp.where(best_g == g, sc_rows[g * 4 + a], cr)
        cand.append(cv)
        craw.append(cr)

    def argmax_first(vals):
        bi = jnp.zeros((1, t), jnp.int32)
        bv = vals[0]
        for a in range(1, EXPERTS_PER_GROUP):
            better = vals[a] > bv
            bi = jnp.where(better, a, bi)
            bv = jnp.where(better, vals[a], bv)
        return bi

    i1 = argmax_first(cand)
    i2 = argmax_first([jnp.where(i1 == a, -jnp.inf, cand[a]) for a in range(EXPERTS_PER_GROUP)])

    def pick(vals, idx):
        out = vals[0]
        for a in range(1, EXPERTS_PER_GROUP):
            out = jnp.where(idx == a, vals[a], out)
        return out

    w1, w2 = pick(craw, i1), pick(craw, i2)
    tot = w1 + w2
    return (best_g * 4 + i1, best_g * 4 + i2), (w1 / tot, w2 / tot)


def _merge_kernel(h_ref, ga_ref, gb_ref, gc_ref, av_ref, ab_ref, ac_ref,
                  avp_ref, acp_ref, avn_ref, acn_ref, yb_ref, yc_ref,
                  cw_ref, wa_ref, wb_ref, wc_ref, wo_ref, mod_ref, g2_ref, rw_ref, rb_ref,
                  h1_ref, m2_ref, ids_ref, wts_ref, *, nt):
    i = pl.program_id(0)
    ti = i % nt
    row = lax.broadcasted_iota(jnp.int32, (TILE, 1), 0)
    has_prev = ti > 1
    has_next = jnp.logical_and(ti > 0, ti < nt - 1)

    v = ac_ref[...].astype(F32) * av_ref[...].astype(F32)
    vp = acp_ref[15:16, :].astype(F32) * avp_ref[15:16, :].astype(F32)
    vn = acn_ref[0:1, :].astype(F32) * avn_ref[0:1, :].astype(F32)
    vp = jnp.where(has_prev, vp, 0.0)
    vn = jnp.where(has_next, vn, 0.0)
    v_prev = jnp.where(row == 0, vp, pltpu.roll(v, 1, axis=0))
    v_next = jnp.where(row == TILE - 1, vn, pltpu.roll(v, TILE - 1, axis=0))
    conv = v_prev * cw_ref[0:1, :] + v * cw_ref[1:2, :] + v_next * cw_ref[2:3, :]
    xa = (ab_ref[...].astype(F32) * conv).astype(BF16)

    halves = [(r, r + TILE // 2) for r in (0, TILE // 2)]
    tn = (((0,), (0,)), ((), ()))
    branch = []
    for lo, hi in halves:
        ya = jnp.dot(xa[lo:hi], wa_ref[...], preferred_element_type=F32)
        yb = jnp.dot(yb_ref[lo:hi, :], wb_ref[...], preferred_element_type=F32)
        yc = lax.dot_general(yc_ref[:, lo:hi], wc_ref[...], tn, preferred_element_type=F32)
        branch.append((ya, yb, yc))
    outs = []
    for (lo, hi), (ya, yb, yc) in zip(halves, branch):
        m = (jax.nn.sigmoid(ga_ref[lo:hi, :].astype(F32)) * ya
             + jax.nn.sigmoid(gb_ref[lo:hi, :].astype(F32)) * yb
             + jax.nn.sigmoid(gc_ref[lo:hi, :].astype(F32)) * yc)
        outs.append(jnp.dot(m.astype(BF16), wo_ref[...], preferred_element_type=F32))
    logits = []
    for (lo, hi), out in zip(halves, outs):
        h1 = h_ref[lo:hi, :] + mod_ref[0, 0, 2:3, :] * out
        h1_ref[lo:hi, :] = h1
        y = h1 * lax.rsqrt(jnp.mean(h1 * h1, axis=-1, keepdims=True) + EPS) * g2_ref[...]
        m2 = y * (1.0 + mod_ref[0, 0, 4:5, :]) + mod_ref[0, 0, 3:4, :]
        m2_ref[lo:hi, :] = m2
        logits.append(lax.dot_general(rw_ref[...], m2, (((1,), (1,)), ((), ())),
                                      precision=lax.Precision.HIGHEST, preferred_element_type=F32))
    for (lo, hi), logits_t in zip(halves, logits):
        ids, wts = _route(logits_t, rb_ref[...])
        for k in range(2):
            ids_ref[k:k + 1, lo:hi] = ids[k]
            wts_ref[k:k + 1, lo:hi] = wts[k]


def _merge(h, z, yb, yc, conv_w, wa, wb, wc, wo, modtab, g2, rw_t, rb, s_all):
    n = h.shape[0]
    nt = s_all // TILE
    hb = TILE // 16
    last16 = n // 16 - 1

    def zspec(width, col):
        return pl.BlockSpec((TILE, width), lambda i: (i, col // width))

    def halo(col, nxt):
        if nxt:
            return pl.BlockSpec((16, CONV_DIM), lambda i: (jnp.minimum((i + 1) * hb, last16), col // CONV_DIM))
        return pl.BlockSpec((16, CONV_DIM), lambda i: (jnp.maximum(i * hb - 1, 0), col // CONV_DIM))

    def full(shape):
        return pl.BlockSpec(shape, lambda i: (0,) * len(shape))

    kern = functools.partial(_merge_kernel, nt=nt)
    return pl.pallas_call(
        kern,
        out_shape=(
            jax.ShapeDtypeStruct((n, D_MODEL), F32),
            jax.ShapeDtypeStruct((n, D_MODEL), F32),
            jax.ShapeDtypeStruct((2, n), jnp.int32),
            jax.ShapeDtypeStruct((2, n), F32),
        ),
        grid=(n // TILE,),
        in_specs=[
            pl.BlockSpec((TILE, D_MODEL), lambda i: (i, 0)),
            zspec(D_MODEL, COL_GATE), zspec(D_MODEL, COL_GATE + D_MODEL), zspec(D_MODEL, COL_GATE + 2 * D_MODEL),
            zspec(CONV_DIM, COL_A_VAL), zspec(CONV_DIM, COL_A_B), zspec(CONV_DIM, COL_A_C),
            halo(COL_A_VAL, False), halo(COL_A_C, False), halo(COL_A_VAL, True), halo(COL_A_C, True),
            pl.BlockSpec((TILE, HGRN_DIM), lambda i: (i, 0)),
            pl.BlockSpec((ATT_DIM, TILE), lambda i: (0, i)),
            full((3, CONV_DIM)),
            full((CONV_DIM, D_MODEL)), full((HGRN_DIM, D_MODEL)), full((ATT_DIM, D_MODEL)),
            full((D_MODEL, D_MODEL)),
            pl.BlockSpec((1, 1, 6, D_MODEL), lambda i: (i // nt, jnp.minimum(i % nt, 1), 0, 0)),
            full((1, D_MODEL)),
            full((N_EXPERTS, D_MODEL)),
            full((N_EXPERTS, 1)),
        ],
        out_specs=(
            pl.BlockSpec((TILE, D_MODEL), lambda i: (i, 0)),
            pl.BlockSpec((TILE, D_MODEL), lambda i: (i, 0)),
            pl.BlockSpec((2, TILE), lambda i: (0, i)),
            pl.BlockSpec((2, TILE), lambda i: (0, i)),
        ),
        compiler_params=_cparams(("arbitrary",)),
        name="merge_route",
    )(h, z, z, z, z, z, z, z, z, z, z, yb, yc, conv_w, wa, wb, wc, wo, modtab, g2, rw_t, rb)


def _dispatch_kernel(dest_ref, m2_ref, xs_in, xs_hbm, dsm, sem_s, sem):
    del xs_in
    cp = pltpu.make_async_copy(dest_ref.at[0, 0], dsm, sem_s)
    cp.start()
    cp.wait()

    def row_copy(t, k):
        return pltpu.make_async_copy(m2_ref.at[pl.ds(t, 1)],
                                     xs_hbm.at[pl.ds(dsm[k * TILE + t], 1)], sem)

    def issue(t, carry):
        row_copy(t, 0).start()
        row_copy(t, 1).start()
        return carry

    lax.fori_loop(0, TILE, issue, 0, unroll=DMA_UNROLL)

    for _ in range(2):
        pltpu.make_async_copy(m2_ref, xs_hbm.at[pl.ds(0, TILE)], sem).wait()


def _dispatch(dest_tiles, m2, xs_init):
    n = m2.shape[0]
    return pl.pallas_call(
        _dispatch_kernel,
        out_shape=jax.ShapeDtypeStruct(xs_init.shape, F32),
        grid=(n // TILE,),
        in_specs=[
            pl.BlockSpec((1, 1, 2 * TILE), lambda i: (i, 0, 0)),
            pl.BlockSpec((TILE, D_MODEL), lambda i: (i, 0)),
            pl.BlockSpec(memory_space=pl.ANY),
        ],
        out_specs=pl.BlockSpec(memory_space=pl.ANY),
        scratch_shapes=[pltpu.SMEM((2 * TILE,), jnp.int32), pltpu.SemaphoreType.DMA, pltpu.SemaphoreType.DMA],
        input_output_aliases={2: 0},
        compiler_params=_cparams(("arbitrary",)),
        name="moe_dispatch",
    )(dest_tiles, m2, xs_init)


def _ffn_kernel(be_ref, bs_ref, nv_ref, x_ref, wg_ref, wu_ref, wd_ref, y_ref):
    del be_ref, bs_ref
    valid = pl.program_id(0) < nv_ref[0]

    @pl.when(valid)
    def _():
        x = x_ref[...].astype(BF16)
        hg = jnp.dot(x, wg_ref[0], preferred_element_type=F32)
        hu = jnp.dot(x, wu_ref[0], preferred_element_type=F32)
        hid = (_silu(hg) * hu).astype(BF16)
        y_ref[...] = jnp.dot(hid, wd_ref[0], preferred_element_type=F32)

    @pl.when(jnp.logical_not(valid))
    def _():
        y_ref[...] = jnp.zeros(y_ref.shape, F32)


def _expert_ffn(blk_e, blk_src, nvalid, xs, wg, wu, wd):
    p = xs.shape[0]
    nb = p // MOE_BM
    grid_spec = pltpu.PrefetchScalarGridSpec(
        num_scalar_prefetch=3,
        grid=(nb,),
        in_specs=[
            pl.BlockSpec((MOE_BM, D_MODEL), lambda i, be, bs, nv: (bs[i], 0)),
            pl.BlockSpec((1, D_MODEL, EXPERT_DFF), lambda i, be, bs, nv: (be[i], 0, 0)),
            pl.BlockSpec((1, D_MODEL, EXPERT_DFF), lambda i, be, bs, nv: (be[i], 0, 0)),
            pl.BlockSpec((1, EXPERT_DFF, D_MODEL), lambda i, be, bs, nv: (be[i], 0, 0)),
        ],
        out_specs=pl.BlockSpec((MOE_BM, D_MODEL), lambda i, be, bs, nv: (i, 0)),
    )
    return pl.pallas_call(
        _ffn_kernel,
        out_shape=jax.ShapeDtypeStruct((p, D_MODEL), F32),
        grid_spec=grid_spec,
        compiler_params=_cparams(("arbitrary",)),
        name="expert_ffn",
    )(blk_e, blk_src, nvalid, xs, wg, wu, wd)


def _combine_kernel(dest_ref, h_ref, w_ref, mod_ref, fg_ref, y_hbm, o_ref, dsm, ybuf, sem_s, sem, *, final):
    cp = pltpu.make_async_copy(dest_ref.at[0, 0], dsm, sem_s)
    cp.start()
    cp.wait()

    def row_copy(t, k):
        return pltpu.make_async_copy(y_hbm.at[pl.ds(dsm[k * TILE + t], 1)],
                                     ybuf.at[k, pl.ds(t, 1)], sem)

    def issue(t, carry):
        row_copy(t, 0).start()
        row_copy(t, 1).start()
        return carry

    lax.fori_loop(0, TILE, issue, 0, unroll=DMA_UNROLL)

    for k in range(2):
        pltpu.make_async_copy(y_hbm.at[pl.ds(0, TILE)], ybuf.at[k], sem).wait()

    w = w_ref[...]
    moe = w[:, 0:1] * ybuf[0] + w[:, 1:2] * ybuf[1]
    h2 = h_ref[...] + mod_ref[0, 0, 5:6, :] * moe
    if final:
        h2 = h2 * lax.rsqrt(jnp.mean(h2 * h2, axis=-1, keepdims=True) + EPS) * fg_ref[...]
    o_ref[...] = h2


def _combine(dest_tiles, h1, wts_col, modtab, final_g, y, batch, s_all, final):
    nt = s_all // TILE
    skip = 1 if final else 0
    nto = nt - skip

    def src(i):
        return (i // nto) * nt + i % nto + skip

    kern = functools.partial(_combine_kernel, final=final)
    return pl.pallas_call(
        kern,
        out_shape=jax.ShapeDtypeStruct((batch * nto * TILE, D_MODEL), F32),
        grid=(batch * nto,),
        in_specs=[
            pl.BlockSpec((1, 1, 2 * TILE), lambda i: (src(i), 0, 0)),
            pl.BlockSpec((TILE, D_MODEL), lambda i: (src(i), 0)),
            pl.BlockSpec((TILE, 2), lambda i: (src(i), 0)),
            pl.BlockSpec((1, 1, 6, D_MODEL), lambda i: (i // nto, jnp.minimum(i % nto + skip, 1), 0, 0)),
            pl.BlockSpec((1, D_MODEL), lambda i: (0, 0)),
            pl.BlockSpec(memory_space=pl.ANY),
        ],
        out_specs=pl.BlockSpec((TILE, D_MODEL), lambda i: (i, 0)),
        scratch_shapes=[
            pltpu.SMEM((2 * TILE,), jnp.int32),
            pltpu.VMEM((2, TILE, D_MODEL), F32),
            pltpu.SemaphoreType.DMA,
            pltpu.SemaphoreType.DMA,
        ],
        compiler_params=_cparams(("arbitrary",)),
        name="moe_combine",
    )(dest_tiles, h1, wts_col, modtab, final_g, y)


def _dispatch_plan(ids, n):
    flat_e = ids.reshape(-1)
    onehot = (flat_e[:, None] == jnp.arange(N_EXPERTS, dtype=jnp.int32)[None, :]).astype(jnp.int32)
    csum = jnp.cumsum(onehot, axis=0)
    counts = csum[-1]
    padded = (counts + MOE_BM - 1) // MOE_BM * MOE_BM
    pend = jnp.cumsum(padded)
    pstart = pend - padded
    dest = jnp.sum(onehot * (pstart[None, :] + csum - 1), axis=1).astype(jnp.int32)
    nb = -(-(2 * n) // MOE_BM) + N_EXPERTS
    nvalid = (pend[-1] // MOE_BM).astype(jnp.int32)
    blk = jnp.arange(nb, dtype=jnp.int32)
    blk_src = jnp.minimum(blk, nvalid - 1)
    blk_e = jnp.minimum(jnp.sum((blk_src[:, None] * MOE_BM >= pend[None, :]).astype(jnp.int32), axis=1),
                        N_EXPERTS - 1).astype(jnp.int32)
    dest2 = dest.reshape(2, n // TILE, TILE)
    dest_tiles = jnp.transpose(dest2, (1, 0, 2)).reshape(n // TILE, 1, 2 * TILE)
    return dest_tiles, blk_e, blk_src, nvalid.reshape(1), nb


def _reorder_w_in(w):
    rest = w[:, 0:4864]
    gates = w[:, 4864:W_IN_USED]
    pad = jnp.zeros((w.shape[0], W_IN_COLS - W_IN_USED), w.dtype)
    return jnp.concatenate([gates, rest, pad], axis=1).astype(BF16)


def _rope_tables(rows, ctx):
    row = jnp.repeat(jnp.arange(rows), GRID_W).astype(F32)
    col = (jnp.arange(rows * GRID_W) % GRID_W).astype(F32)
    inv = ROPE_THETA ** (-jnp.arange(0, ROPE_AXIS_DIM, 2, dtype=F32) / ROPE_AXIS_DIM)
    ar = row[:, None] * inv
    ac = col[:, None] * inv
    ang = jnp.concatenate([ar, ar, ac, ac], axis=-1)
    cos = jnp.concatenate([jnp.ones((ctx, HEAD_DIM), F32), jnp.cos(ang)], axis=0)
    sin = jnp.concatenate([jnp.zeros((ctx, HEAD_DIM), F32), jnp.sin(ang)], axis=0)
    first = (jnp.arange(HEAD_DIM) % ROPE_AXIS_DIM) < (ROPE_AXIS_DIM // 2)
    sa = jnp.where(first[None, :], -sin, 0.0)
    sb = jnp.where(first[None, :], 0.0, sin)
    rep = ATT_DIM // HEAD_DIM
    return jnp.tile(cos, (1, rep)), jnp.tile(sa, (1, rep)), jnp.tile(sb, (1, rep))


def kernel(x, c, ctx, c_ctx, w_mod, b_mod, norm1_g, norm2_g, w_in, conv_w, lb_param, hgrn_norm_g,
           q_norm_g, k_norm_g, w_a_out, w_b_out, w_c_out, w_o, router_w, router_b, w_gate, w_up,
           w_down, final_g):
    batch, seq, _ = x.shape
    n_ctx = ctx.shape[1]
    depth = w_mod.shape[0]
    assert n_ctx == TILE and seq % TILE == 0 and seq % GRID_W == 0 and batch + 1 <= 8
    s_all = n_ctx + seq
    n = batch * s_all

    h = jnp.concatenate([ctx, x], axis=1).reshape(n, D_MODEL)

    cs = jnp.zeros((8, D_MODEL), F32).at[:batch].set(c).at[batch].set(c_ctx)
    mods = _modulation(cs, w_mod, b_mod)
    lat = mods[:, :batch].reshape(depth, batch, 6, D_MODEL)
    cxt = jnp.broadcast_to(mods[:, batch].reshape(depth, 1, 6, D_MODEL), lat.shape)
    modtabs = jnp.stack([cxt, lat], axis=2)

    p = jax.nn.softmax(lb_param.astype(F32), axis=0)
    cum = jnp.cumsum(p, axis=0)
    lower = cum - cum[0:1]

    cos, sa, sb = _rope_tables(seq // GRID_W, n_ctx)
    seg = jnp.arange(ATT_DIM) // HEAD_DIM
    bd = (seg[:, None] == seg[None, :]).astype(BF16)
    rw_t = router_w.T.astype(F32)
    rb = router_b.reshape(N_EXPERTS, 1).astype(F32)
    fg = final_g.reshape(1, D_MODEL)

    out = None
    for layer in range(depth):
        modtab = modtabs[layer]
        z = _in_proj(h, modtab, norm1_g[layer].reshape(1, D_MODEL), _reorder_w_in(w_in[layer]), s_all, n_ctx)

        lbt = jnp.transpose(lower[layer].reshape(2, HGRN_HEADS, HGRN_K), (1, 0, 2))
        ngt = hgrn_norm_g[layer].reshape(HGRN_HEADS, 1, HGRN_K)
        sf, sbk = _hgrn_states(z, lbt, batch, s_all)
        yb = _hgrn_out(z, sf, sbk, lbt, ngt, batch, s_all)

        qg = jnp.tile(q_norm_g[layer], ATT_HEADS).reshape(1, ATT_DIM)
        kg = jnp.tile(k_norm_g[layer], ATT_HEADS).reshape(1, ATT_DIM)
        q_t, k_hm, v_t = _qk_prep(z, cos, sa, sb, qg, kg, bd, s_all)
        score_bound = (HEAD_DIM * HEAD_DIM ** -0.5 * LOG2E * ATT_BOUND_MARGIN
                       * jnp.max(jnp.abs(q_norm_g[layer])) * jnp.max(jnp.abs(k_norm_g[layer])))
        yc = lax.cond(score_bound <= ATT_SAFE_LOG2,
                      lambda a, b, c_: _attention(a, b, c_, batch, s_all, n_ctx, False),
                      lambda a, b, c_: _attention(a, b, c_, batch, s_all, n_ctx, True),
                      q_t, k_hm, v_t)

        h1, m2, ids, wts = _merge(
            h, z, yb, yc, conv_w[layer], w_a_out[layer].astype(BF16), w_b_out[layer].astype(BF16),
            w_c_out[layer].astype(BF16), w_o[layer].astype(BF16), modtab,
            norm2_g[layer].reshape(1, D_MODEL), rw_t, rb, s_all)

        dest_tiles, blk_e, blk_src, nvalid, nb = _dispatch_plan(ids, n)
        xs = _dispatch(dest_tiles, m2, jnp.zeros((nb * MOE_BM, D_MODEL), F32))
        y = _expert_ffn(blk_e, blk_src, nvalid, xs, w_gate[layer].astype(BF16), w_up[layer].astype(BF16),
                        w_down[layer].astype(BF16))
        wts_col = wts.T
        last = layer == depth - 1
        res = _combine(dest_tiles, h1, wts_col, modtab, fg, y, batch, s_all, last)
        if last:
            out = res.reshape(batch, seq, D_MODEL)
        else:
            h = res
    return out
```

```python
import functools

import jax
import jax.numpy as jnp
from jax import lax
from jax.experimental import pallas as pl
from jax.experimental.pallas import tpu as pltpu

F32 = jnp.float32
BF16 = jnp.bfloat16

D_MODEL = 1024
EPS = 1e-6
GRID_W = 64
CONV_DIM = 512
HGRN_DIM = 512
HGRN_HEADS = 4
HGRN_K = 128
SUB = 16
HCHUNK = 64
ATT_HEADS = 8
KV_HEADS = 2
ATT_GROUP = 4
HEAD_DIM = 64
ATT_DIM = 512
ROPE_THETA = 10000.0
ROPE_AXIS_DIM = 32
N_EXPERTS = 16
N_GROUPS = 4
EXPERTS_PER_GROUP = 4
EXPERT_DFF = 1024
MOE_BM = 256
DMA_UNROLL = 8

TILE = 256
VMEM_LIMIT = 48 * 1024 * 1024

COL_GATE = 0
COL_A_VAL, COL_A_B, COL_A_C = 3072, 3584, 4096
COL_B_Q, COL_B_FF, COL_B_FB, COL_B_I, COL_B_G = 4608, 5120, 5632, 6144, 6656
COL_C_Q, COL_C_K, COL_C_V = 7168, 7680, 7808
W_IN_USED = 7936
W_IN_COLS = 8192
IN_TN = 512
KV_DIM = KV_HEADS * HEAD_DIM
VT_ROWS = HEAD_DIM + 16
ATT_SAFE_LOG2 = 60.0
ATT_BOUND_MARGIN = 1.02
LOG2E = 1.4426950408889634


def _cparams(sem):
    return pltpu.CompilerParams(dimension_semantics=sem, vmem_limit_bytes=VMEM_LIMIT)


def _silu(x):
    return x * jax.nn.sigmoid(x)


def _mod_kernel(c_ref, w_ref, b_ref, o_ref):
    a = _silu(c_ref[...])
    o_ref[0] = jnp.dot(a, w_ref[0], precision=lax.Precision.HIGHEST,
                       preferred_element_type=F32) + b_ref[0]


def _modulation(cs, w_mod, b_mod):
    depth = w_mod.shape[0]
    ncol = w_mod.shape[2]
    tn = 1024
    return pl.pallas_call(
        _mod_kernel,
        out_shape=jax.ShapeDtypeStruct((depth, 8, ncol), F32),
        grid=(depth, ncol // tn),
        in_specs=[
            pl.BlockSpec((8, D_MODEL), lambda l, j: (0, 0)),
            pl.BlockSpec((1, D_MODEL, tn), lambda l, j: (l, 0, j)),
            pl.BlockSpec((1, 1, tn), lambda l, j: (l, 0, j)),
        ],
        out_specs=pl.BlockSpec((1, 8, tn), lambda l, j: (l, 0, j)),
        compiler_params=_cparams(("arbitrary", "arbitrary")),
        name="modulation",
    )(cs, w_mod, b_mod.reshape(depth, 1, ncol))


def _in_proj_kernel(h_ref, mod_ref, g_ref, w_ref, z_ref, u_ref, *, tm, s_all, ctx):
    i = pl.program_id(0)
    j = pl.program_id(1)

    @pl.when(j == 0)
    def _():
        x = h_ref[...]
        y = x * lax.rsqrt(jnp.mean(x * x, axis=-1, keepdims=True) + EPS) * g_ref[...]
        pos = (i * tm) % s_all + lax.broadcasted_iota(jnp.int32, (tm, 1), 0)
        is_ctx = pos < ctx
        sh = jnp.where(is_ctx, mod_ref[0, 0, 0:1, :], mod_ref[0, 1, 0:1, :])
        sc = jnp.where(is_ctx, mod_ref[0, 0, 1:2, :], mod_ref[0, 1, 1:2, :])
        u_ref[...] = (y * (1.0 + sc) + sh).astype(BF16)

    z_ref[...] = jnp.dot(u_ref[...], w_ref[...], preferred_element_type=F32).astype(BF16)


def _in_proj(h, modtab, g, w_in, s_all, ctx):
    n = h.shape[0]
    tm = next(t for t in (1280, 768, 512, 256) if s_all % t == 0)
    kern = functools.partial(_in_proj_kernel, tm=tm, s_all=s_all, ctx=ctx)
    return pl.pallas_call(
        kern,
        out_shape=jax.ShapeDtypeStruct((n, W_IN_COLS), BF16),
        grid=(n // tm, W_IN_COLS // IN_TN),
        in_specs=[
            pl.BlockSpec((tm, D_MODEL), lambda i, j: (i, 0)),
            pl.BlockSpec((1, 2, 6, D_MODEL), lambda i, j: ((i * tm) // s_all, 0, 0, 0)),
            pl.BlockSpec((1, D_MODEL), lambda i, j: (0, 0)),
            pl.BlockSpec((D_MODEL, IN_TN), lambda i, j: (0, j)),
        ],
        out_specs=pl.BlockSpec((tm, IN_TN), lambda i, j: (i, j)),
        scratch_shapes=[pltpu.VMEM((tm, D_MODEL), BF16)],
        compiler_params=_cparams(("arbitrary", "arbitrary")),
        name="in_proj",
    )(h, modtab, g, w_in)


def _seg_mean_sq(x, bd):
    sq = x * x
    hi = sq.astype(BF16)
    lo = (sq - hi.astype(F32)).astype(BF16)
    s = jnp.dot(hi, bd, preferred_element_type=F32) + jnp.dot(lo, bd, preferred_element_type=F32)
    return s * (1.0 / HEAD_DIM)


def _qk_kernel(q_ref, k_ref, v_ref, cos_ref, sa_ref, sb_ref, qg_ref, kg_ref, bd_ref,
               qt_ref, ko_ref, vt_ref):
    bd = bd_ref[...]

    def norm_rope(x, g, width):
        y = x * lax.rsqrt(_seg_mean_sq(x, bd[0:width, 0:width]) + EPS) * g
        up = pltpu.roll(y, width - 16, axis=1)
        dn = pltpu.roll(y, 16, axis=1)
        return y * cos_ref[:, 0:width] + up * sa_ref[:, 0:width] + dn * sb_ref[:, 0:width]

    q = norm_rope(q_ref[...].astype(F32), qg_ref[...], ATT_DIM)
    qt_ref[...] = jnp.transpose(q * (HEAD_DIM ** -0.5 * LOG2E)).astype(BF16)
    k = norm_rope(k_ref[...].astype(F32), kg_ref[:, 0:KV_DIM], KV_DIM)
    for hd in range(KV_HEADS):
        ko_ref[hd] = k[:, hd * HEAD_DIM:(hd + 1) * HEAD_DIM].astype(BF16)
    vt = jnp.transpose(v_ref[...].astype(F32)).astype(BF16)
    for hd in range(KV_HEADS):
        vt_ref[hd, 0:HEAD_DIM, :] = vt[hd * HEAD_DIM:(hd + 1) * HEAD_DIM, :]
        vt_ref[hd, HEAD_DIM:VT_ROWS, :] = jnp.ones((VT_ROWS - HEAD_DIM, TILE), BF16)


def _qk_prep(z, cos, sa, sb, qg, kg, bd, s_all):
    n = z.shape[0]
    nt = s_all // TILE
    tab = pl.BlockSpec((TILE, ATT_DIM), lambda i: (i % nt, 0))
    vec = pl.BlockSpec((1, ATT_DIM), lambda i: (0, 0))
    return pl.pallas_call(
        _qk_kernel,
        out_shape=(
            jax.ShapeDtypeStruct((ATT_DIM, n), BF16),
            jax.ShapeDtypeStruct((KV_HEADS, n, HEAD_DIM), BF16),
            jax.ShapeDtypeStruct((KV_HEADS, VT_ROWS, n), BF16),
        ),
        grid=(n // TILE,),
        in_specs=[
            pl.BlockSpec((TILE, ATT_DIM), lambda i: (i, COL_C_Q // ATT_DIM)),
            pl.BlockSpec((TILE, KV_DIM), lambda i: (i, COL_C_K // KV_DIM)),
            pl.BlockSpec((TILE, KV_DIM), lambda i: (i, COL_C_V // KV_DIM)),
            tab, tab, tab, vec, vec,
            pl.BlockSpec((ATT_DIM, ATT_DIM), lambda i: (0, 0)),
        ],
        out_specs=(
            pl.BlockSpec((ATT_DIM, TILE), lambda i: (0, i)),
            pl.BlockSpec((KV_HEADS, TILE, HEAD_DIM), lambda i: (0, i, 0)),
            pl.BlockSpec((KV_HEADS, VT_ROWS, TILE), lambda i: (0, 0, i)),
        ),
        compiler_params=_cparams(("arbitrary",)),
        name="qk_prep",
    )(z, z, z, cos, sa, sb, qg, kg, bd)


def _attn_kernel(qt_ref, k_ref, vt_ref, o_ref, qs_ref, m_ref, acc_ref, *, tq, tk, nk, ctx, online):
    qi = pl.program_id(2)
    ki = pl.program_id(3)
    is_ctx = qi == 0

    @pl.when(ki == 0)
    def _():
        for g in range(ATT_GROUP):
            qs_ref[:, g * tq:(g + 1) * tq] = qt_ref[g * HEAD_DIM:(g + 1) * HEAD_DIM, :]
        m_ref[...] = jnp.full(m_ref.shape, -jnp.inf, F32)
        acc_ref[...] = jnp.zeros(acc_ref.shape, F32)

    def scores(j):
        k = k_ref[0, j * ctx:(j + 1) * ctx, :]
        return jnp.dot(k, qs_ref[...], preferred_element_type=F32)

    def update(j, s):
        vt = vt_ref[0, :, j * ctx:(j + 1) * ctx]
        if online:
            m_old = m_ref[...]
            m_new = jnp.maximum(m_old, jnp.max(s, axis=0, keepdims=True))
            p = jnp.exp2(s - m_new)
            alpha = jnp.exp2(m_old - m_new)
            m_ref[...] = m_new
            pv = jnp.dot(vt, p.astype(BF16), preferred_element_type=F32)
            acc_ref[...] = acc_ref[...] * alpha + pv
        else:
            pv = jnp.dot(vt, jnp.exp2(s).astype(BF16), preferred_element_type=F32)
            acc_ref[...] = acc_ref[...] + pv

    @pl.when(jnp.logical_and(is_ctx, ki == 0))
    def _():
        update(0, scores(0))

    @pl.when(jnp.logical_not(is_ctx))
    def _():
        nblk = tk // ctx
        ahead = 2
        pending = [scores(j) for j in range(min(ahead, nblk))]
        for j in range(nblk):
            if j + ahead < nblk:
                pending.append(scores(j + ahead))
            update(j, pending.pop(0))

    last = jnp.where(is_ctx, 0, nk - 1)

    @pl.when(ki == last)
    def _():
        out = acc_ref[0:HEAD_DIM, :] / acc_ref[HEAD_DIM:HEAD_DIM + 1, :]
        for g in range(ATT_GROUP):
            o_ref[g * HEAD_DIM:(g + 1) * HEAD_DIM, :] = out[:, g * tq:(g + 1) * tq].astype(BF16)


def _attention(q_t, k_hm, v_t, batch, s_all, ctx, online):
    n = q_t.shape[1]
    tq = TILE
    tk = next(t for t in (3328, 1280, 768, 512, 256) if s_all % t == 0)
    nq, nk = s_all // tq, s_all // tk
    grp = ATT_GROUP * HEAD_DIM

    def kv_idx(ki, qi):
        return jnp.where(qi == 0, 0, ki)

    kern = functools.partial(_attn_kernel, tq=tq, tk=tk, nk=nk, ctx=ctx, online=online)
    return pl.pallas_call(
        kern,
        out_shape=jax.ShapeDtypeStruct((ATT_DIM, n), BF16),
        grid=(batch, KV_HEADS, nq, nk),
        in_specs=[
            pl.BlockSpec((grp, tq), lambda b, h, qi, ki: (h, b * nq + qi)),
            pl.BlockSpec((1, tk, HEAD_DIM), lambda b, h, qi, ki: (h, b * nk + kv_idx(ki, qi), 0)),
            pl.BlockSpec((1, VT_ROWS, tk), lambda b, h, qi, ki: (h, 0, b * nk + kv_idx(ki, qi))),
        ],
        out_specs=pl.BlockSpec((grp, tq), lambda b, h, qi, ki: (h, b * nq + qi)),
        scratch_shapes=[
            pltpu.VMEM((HEAD_DIM, ATT_GROUP * tq), BF16),
            pltpu.VMEM((1, ATT_GROUP * tq), F32),
            pltpu.VMEM((VT_ROWS, ATT_GROUP * tq), F32),
        ],
        compiler_params=_cparams(("arbitrary",) * 4),
        name="attention",
    )(q_t, k_hm, v_t)


def _chunk_cumsum(g, reverse):
    rows = g.shape[0]
    pos = lax.broadcasted_iota(jnp.int32, (rows, 1), 0) % HCHUNK
    b = g
    s = 1
    while s < HCHUNK:
        if reverse:
            shifted = pltpu.roll(b, rows - s, axis=0)
            keep = pos < HCHUNK - s
        else:
            shifted = pltpu.roll(b, s, axis=0)
            keep = pos >= s
        b = b + jnp.where(keep, shifted, 0.0)
        s *= 2
    return b


def _hgrn_gates(f_raw, lb, reverse):
    fg = lb + (1.0 - lb) * jax.nn.sigmoid(f_raw.astype(F32))
    return 1.0 - fg, _chunk_cumsum(jnp.log(fg), reverse)


def _hgrn_state_kernel(ff_ref, fb_ref, vf_ref, vb_ref, lb_ref, sf_ref, sb_ref, stf_ref, stb_ref, *, nchunk):
    i = pl.program_id(1)

    @pl.when(i == 0)
    def _():
        stf_ref[...] = jnp.zeros(stf_ref.shape, F32)
        stb_ref[...] = jnp.zeros(stb_ref.shape, F32)

    for hd in range(HGRN_HEADS):
        cols = slice(hd * HGRN_K, (hd + 1) * HGRN_K)
        for reverse, f_ref, v_ref, out_ref, st_ref in ((False, ff_ref, vf_ref, sf_ref, stf_ref),
                                                        (True, fb_ref, vb_ref, sb_ref, stb_ref)):
            lb = lb_ref[hd, 1:2, :] if reverse else lb_ref[hd, 0:1, :]
            kk, b = _hgrn_gates(f_ref[:, cols], lb, reverse)
            v = v_ref[:, cols]
            st = st_ref[hd]
            order = range(nchunk - 1, -1, -1) if reverse else range(nchunk)
            for c in order:
                r0 = c * HCHUNK
                out_ref[0, hd, c] = st.astype(BF16)
                bc = b[r0:r0 + HCHUNK]
                b_last = bc[0:1] if reverse else bc[HCHUNK - 1:HCHUNK]
                kt = (kk[r0:r0 + HCHUNK] * jnp.exp(b_last - bc)).astype(BF16)
                ds = lax.dot_general(v[r0:r0 + HCHUNK], kt, (((0,), (0,)), ((), ())),
                                     preferred_element_type=F32)
                st = st * jnp.exp(b_last) + ds
            st_ref[hd] = st


def _hgrn_states(z, lbt, batch, s_all):
    nt = s_all // TILE
    nchunk = TILE // HCHUNK
    cff, cfb, cv = COL_B_FF // HGRN_DIM, COL_B_FB // HGRN_DIM, COL_B_I // HGRN_DIM

    def rev(i):
        return jnp.where(i == 0, 0, nt - i)

    st_shape = jax.ShapeDtypeStruct((batch, HGRN_HEADS, nt * nchunk, HGRN_K, HGRN_K), BF16)
    st_scratch = pltpu.VMEM((HGRN_HEADS, HGRN_K, HGRN_K), F32)
    kern = functools.partial(_hgrn_state_kernel, nchunk=nchunk)
    return pl.pallas_call(
        kern,
        out_shape=(st_shape, st_shape),
        grid=(batch, nt),
        in_specs=[
            pl.BlockSpec((TILE, HGRN_DIM), lambda b, i: (b * nt + i, cff)),
            pl.BlockSpec((TILE, HGRN_DIM), lambda b, i: (b * nt + rev(i), cfb)),
            pl.BlockSpec((TILE, HGRN_DIM), lambda b, i: (b * nt + i, cv)),
            pl.BlockSpec((TILE, HGRN_DIM), lambda b, i: (b * nt + rev(i), cv)),
            pl.BlockSpec((HGRN_HEADS, 2, HGRN_K), lambda b, i: (0, 0, 0)),
        ],
        out_specs=(
            pl.BlockSpec((1, HGRN_HEADS, nchunk, HGRN_K, HGRN_K), lambda b, i: (b, 0, i, 0, 0)),
            pl.BlockSpec((1, HGRN_HEADS, nchunk, HGRN_K, HGRN_K), lambda b, i: (b, 0, rev(i), 0, 0)),
        ),
        scratch_shapes=[st_scratch, st_scratch],
        compiler_params=_cparams(("arbitrary",) * 2),
        name="hgrn_states",
    )(z, z, z, z, lbt)


def _hgrn_matmul_terms(qs, gates, v_bf, st_refs, nchunk):
    nsub = HCHUNK // SUB
    nt = (((1,), (1,)), ((), ()))
    inter, scores = {}, {}
    for d, (kk, b) in enumerate(gates):
        reverse = d == 1
        for c in range(nchunk):
            r0 = c * HCHUNK
            qt = (qs[r0:r0 + HCHUNK] * jnp.exp(b[r0:r0 + HCHUNK])).astype(BF16)
            inter[d, c] = lax.dot_general(qt, st_refs[d][0, 0, c], nt, preferred_element_type=F32)
            for i in range(nsub):
                a0 = r0 + i * SUB
                if reverse and i < nsub - 1:
                    lo, hi, ref_row = a0 + SUB, r0 + HCHUNK, a0 + SUB
                elif (not reverse) and i > 0:
                    lo, hi, ref_row = r0, a0, a0 - 1
                else:
                    continue
                b_ref = b[ref_row:ref_row + 1]
                q_sc = (qs[a0:a0 + SUB] * jnp.exp(b[a0:a0 + SUB] - b_ref)).astype(BF16)
                k_sc = (kk[lo:hi] * jnp.exp(b_ref - b[lo:hi])).astype(BF16)
                scores[d, c, i] = (lo, hi, lax.dot_general(q_sc, k_sc, nt, preferred_element_type=F32))
    total = None
    for d in range(len(gates)):
        outs = []
        for c in range(nchunk):
            subs = []
            for i in range(nsub):
                o = inter[d, c][i * SUB:(i + 1) * SUB]
                if (d, c, i) in scores:
                    lo, hi, sc = scores[d, c, i]
                    o = o + jnp.dot(sc.astype(BF16), v_bf[lo:hi], preferred_element_type=F32)
                subs.append(o)
            outs.append(jnp.concatenate(subs, axis=0))
        part = jnp.concatenate(outs, axis=0)
        total = part if total is None else total + part
    return total


def _hgrn_exact_terms(qs, kk, b, v32, reverse):
    t8 = lax.broadcasted_iota(jnp.int32, (8, 1), 0)
    b2 = b * LOG2E
    groups = []
    for a0 in range(0, qs.shape[0], SUB):
        for h0 in range(0, SUB, 8):
            qh = qs[a0 + h0:a0 + h0 + 8]
            bh = b2[a0 + h0:a0 + h0 + 8]
            oh = None
            for s in (range(h0, SUB) if reverse else range(h0 + 8)):
                diff = bh - b2[a0 + s:a0 + s + 1]
                if h0 <= s < h0 + 8:
                    keep = (t8 + h0 <= s) if reverse else (t8 + h0 >= s)
                    diff = jnp.where(keep, diff, -jnp.inf)
                w = jnp.sum(qh * kk[a0 + s:a0 + s + 1] * jnp.exp2(diff), axis=-1, keepdims=True)
                term = w * v32[a0 + s:a0 + s + 1]
                oh = term if oh is None else oh + term
            groups.append(oh)
    return jnp.concatenate(groups, axis=0)


def _hgrn_out_kernel(q_ref, ff_ref, fb_ref, v_ref, gt_ref, sf_ref, sb_ref, lb_ref, ng_ref, y_ref, *, nchunk):
    qs = _silu(q_ref[...].astype(F32))
    v_bf = v_ref[...]
    v32 = v_bf.astype(F32)
    gates = []
    for reverse, f_ref in ((False, ff_ref), (True, fb_ref)):
        lb = lb_ref[0, 1:2, :] if reverse else lb_ref[0, 0:1, :]
        gates.append(_hgrn_gates(f_ref[...], lb, reverse))
    o = _hgrn_matmul_terms(qs, gates, v_bf, (sf_ref, sb_ref), nchunk)
    o = o + _hgrn_exact_terms(qs, gates[0][0], gates[0][1], v32, False)
    o = o + _hgrn_exact_terms(qs, gates[1][0], gates[1][1], v32, True)
    o = o * lax.rsqrt(jnp.mean(o * o, axis=-1, keepdims=True) + EPS) * ng_ref[0]
    y_ref[...] = (o * _silu(gt_ref[...].astype(F32))).astype(BF16)


def _hgrn_out(z, sf, sb, lbt, ngt, batch, s_all):
    n = z.shape[0]
    nt = s_all // TILE
    nchunk = TILE // HCHUNK
    cols = [c // HGRN_K for c in (COL_B_Q, COL_B_FF, COL_B_FB, COL_B_I, COL_B_G)]

    def zspec(c):
        return pl.BlockSpec((TILE, HGRN_K), lambda b, h, i: (b * nt + i, c + h))

    st_spec = pl.BlockSpec((1, 1, nchunk, HGRN_K, HGRN_K), lambda b, h, i: (b, h, i, 0, 0))
    kern = functools.partial(_hgrn_out_kernel, nchunk=nchunk)
    return pl.pallas_call(
        kern,
        out_shape=jax.ShapeDtypeStruct((n, HGRN_DIM), BF16),
        grid=(batch, HGRN_HEADS, nt),
        in_specs=[zspec(c) for c in cols] + [
            st_spec, st_spec,
            pl.BlockSpec((1, 2, HGRN_K), lambda b, h, i: (h, 0, 0)),
            pl.BlockSpec((1, 1, HGRN_K), lambda b, h, i: (h, 0, 0)),
        ],
        out_specs=pl.BlockSpec((TILE, HGRN_K), lambda b, h, i: (b * nt + i, h)),
        compiler_params=_cparams(("arbitrary",) * 3),
        name="hgrn_out",
    )(z, z, z, z, z, sf, sb, lbt, ngt)


def _route(logits_t, rb):
    scores = jax.nn.sigmoid(logits_t)
    sel = scores + rb
    t = logits_t.shape[1]
    rows = [sel[e:e + 1] for e in range(N_EXPERTS)]
    sc_rows = [scores[e:e + 1] for e in range(N_EXPERTS)]

    def group_score(gr):
        best = None
        for a in range(EXPERTS_PER_GROUP):
            for c in range(a + 1, EXPERTS_PER_GROUP):
                pair = gr[a] + gr[c]
                best = pair if best is None else jnp.maximum(best, pair)
        return best

    gs = [group_score(rows[g * 4:(g + 1) * 4]) for g in range(N_GROUPS)]
    best_g = jnp.zeros((1, t), jnp.int32)
    best_v = gs[0]
    for g in range(1, N_GROUPS):
        better = gs[g] > best_v
        best_g = jnp.where(better, g, best_g)
        best_v = jnp.where(better, gs[g], best_v)
    cand, craw = [], []
    for a in range(EXPERTS_PER_GROUP):
        cv, cr = rows[a], sc_rows[a]
        for g in range(1, N_GROUPS):
            cv = jnp.where(best_g == g, rows[g * 4 + a], cv)
            cr = jnp.where(best_g == g, sc_rows[g * 4 + a], cr)
        cand.append(cv)
        craw.append(cr)

    def argmax_first(vals):
        bi = jnp.zeros((1, t), jnp.int32)
        bv = vals[0]
        for a in range(1, EXPERTS_PER_GROUP):
            better = vals[a] > bv
            bi = jnp.where(better, a, bi)
            bv = jnp.where(better, vals[a], bv)
        return bi

    i1 = argmax_first(cand)
    i2 = argmax_first([jnp.where(i1 == a, -jnp.inf, cand[a]) for a in range(EXPERTS_PER_GROUP)])

    def pick(vals, idx):
        out = vals[0]
        for a in range(1, EXPERTS_PER_GROUP):
            out = jnp.where(idx == a, vals[a], out)
        return out

    w1, w2 = pick(craw, i1), pick(craw, i2)
    tot = w1 + w2
    return (best_g * 4 + i1, best_g * 4 + i2), (w1 / tot, w2 / tot)


def _merge_kernel(h_ref, ga_ref, gb_ref, gc_ref, av_ref, ab_ref, ac_ref,
                  avp_ref, acp_ref, avn_ref, acn_ref, yb_ref, yc_ref,
                  cw_ref, wa_ref, wb_ref, wc_ref, wo_ref, mod_ref, g2_ref, rw_ref, rb_ref,
                  h1_ref, m2_ref, ids_ref, wts_ref, *, nt):
    i = pl.program_id(0)
    ti = i % nt
    row = lax.broadcasted_iota(jnp.int32, (TILE, 1), 0)
    has_prev = ti > 1
    has_next = jnp.logical_and(ti > 0, ti < nt - 1)

    v = ac_ref[...].astype(F32) * av_ref[...].astype(F32)
    vp = acp_ref[15:16, :].astype(F32) * avp_ref[15:16, :].astype(F32)
    vn = acn_ref[0:1, :].astype(F32) * avn_ref[0:1, :].astype(F32)
    vp = jnp.where(has_prev, vp, 0.0)
    vn = jnp.where(has_next, vn, 0.0)
    v_prev = jnp.where(row == 0, vp, pltpu.roll(v, 1, axis=0))
    v_next = jnp.where(row == TILE - 1, vn, pltpu.roll(v, TILE - 1, axis=0))
    conv = v_prev * cw_ref[0:1, :] + v * cw_ref[1:2, :] + v_next * cw_ref[2:3, :]
    xa = (ab_ref[...].astype(F32) * conv).astype(BF16)

    halves = [(r, r + TILE // 2) for r in (0, TILE // 2)]
    tn = (((0,), (0,)), ((), ()))
    branch = []
    for lo, hi in halves:
        ya = jnp.dot(xa[lo:hi], wa_ref[...], preferred_element_type=F32)
        yb = jnp.dot(yb_ref[lo:hi, :], wb_ref[...], preferred_element_type=F32)
        yc = lax.dot_general(yc_ref[:, lo:hi], wc_ref[...], tn, preferred_element_type=F32)
        branch.append((ya, yb, yc))
    outs = []
    for (lo, hi), (ya, yb, yc) in zip(halves, branch):
        m = (jax.nn.sigmoid(ga_ref[lo:hi, :].astype(F32)) * ya
             + jax.nn.sigmoid(gb_ref[lo:hi, :].astype(F32)) * yb
             + jax.nn.sigmoid(gc_ref[lo:hi, :].astype(F32)) * yc)
        outs.append(jnp.dot(m.astype(BF16), wo_ref[...], preferred_element_type=F32))
    logits = []
    for (lo, hi), out in zip(halves, outs):
        h1 = h_ref[lo:hi, :] + mod_ref[0, 0, 2:3, :] * out
        h1_ref[lo:hi, :] = h1
        y = h1 * lax.rsqrt(jnp.mean(h1 * h1, axis=-1, keepdims=True) + EPS) * g2_ref[...]
        m2 = y * (1.0 + mod_ref[0, 0, 4:5, :]) + mod_ref[0, 0, 3:4, :]
        m2_ref[lo:hi, :] = m2
        logits.append(lax.dot_general(rw_ref[...], m2, (((1,), (1,)), ((), ())),
                                      precision=lax.Precision.HIGHEST, preferred_element_type=F32))
    for (lo, hi), logits_t in zip(halves, logits):
        ids, wts = _route(logits_t, rb_ref[...])
        for k in range(2):
            ids_ref[k:k + 1, lo:hi] = ids[k]
            wts_ref[k:k + 1, lo:hi] = wts[k]


def _merge(h, z, yb, yc, conv_w, wa, wb, wc, wo, modtab, g2, rw_t, rb, s_all):
    n = h.shape[0]
    nt = s_all // TILE
    hb = TILE // 16
    last16 = n // 16 - 1

    def zspec(width, col):
        return pl.BlockSpec((TILE, width), lambda i: (i, col // width))

    def halo(col, nxt):
        if nxt:
            return pl.BlockSpec((16, CONV_DIM), lambda i: (jnp.minimum((i + 1) * hb, last16), col // CONV_DIM))
        return pl.BlockSpec((16, CONV_DIM), lambda i: (jnp.maximum(i * hb - 1, 0), col // CONV_DIM))

    def full(shape):
        return pl.BlockSpec(shape, lambda i: (0,) * len(shape))

    kern = functools.partial(_merge_kernel, nt=nt)
    return pl.pallas_call(
        kern,
        out_shape=(
            jax.ShapeDtypeStruct((n, D_MODEL), F32),
            jax.ShapeDtypeStruct((n, D_MODEL), F32),
            jax.ShapeDtypeStruct((2, n), jnp.int32),
            jax.ShapeDtypeStruct((2, n), F32),
        ),
        grid=(n // TILE,),
        in_specs=[
            pl.BlockSpec((TILE, D_MODEL), lambda i: (i, 0)),
            zspec(D_MODEL, COL_GATE), zspec(D_MODEL, COL_GATE + D_MODEL), zspec(D_MODEL, COL_GATE + 2 * D_MODEL),
            zspec(CONV_DIM, COL_A_VAL), zspec(CONV_DIM, COL_A_B), zspec(CONV_DIM, COL_A_C),
            halo(COL_A_VAL, False), halo(COL_A_C, False), halo(COL_A_VAL, True), halo(COL_A_C, True),
            pl.BlockSpec((TILE, HGRN_DIM), lambda i: (i, 0)),
            pl.BlockSpec((ATT_DIM, TILE), lambda i: (0, i)),
            full((3, CONV_DIM)),
            full((CONV_DIM, D_MODEL)), full((HGRN_DIM, D_MODEL)), full((ATT_DIM, D_MODEL)),
            full((D_MODEL, D_MODEL)),
            pl.BlockSpec((1, 1, 6, D_MODEL), lambda i: (i // nt, jnp.minimum(i % nt, 1), 0, 0)),
            full((1, D_MODEL)),
            full((N_EXPERTS, D_MODEL)),
            full((N_EXPERTS, 1)),
        ],
        out_specs=(
            pl.BlockSpec((TILE, D_MODEL), lambda i: (i, 0)),
            pl.BlockSpec((TILE, D_MODEL), lambda i: (i, 0)),
            pl.BlockSpec((2, TILE), lambda i: (0, i)),
            pl.BlockSpec((2, TILE), lambda i: (0, i)),
        ),
        compiler_params=_cparams(("arbitrary",)),
        name="merge_route",
    )(h, z, z, z, z, z, z, z, z, z, z, yb, yc, conv_w, wa, wb, wc, wo, modtab, g2, rw_t, rb)


def _dispatch_kernel(dest_ref, m2_ref, xs_in, xs_hbm, dsm, sem_s, sem):
    del xs_in
    cp = pltpu.make_async_copy(dest_ref.at[0, 0], dsm, sem_s)
    cp.start()
    cp.wait()

    def row_copy(t, k):
        return pltpu.make_async_copy(m2_ref.at[pl.ds(t, 1)],
                                     xs_hbm.at[pl.ds(dsm[k * TILE + t], 1)], sem)

    def issue(t, carry):
        row_copy(t, 0).start()
        row_copy(t, 1).start()
        return carry

    lax.fori_loop(0, TILE, issue, 0, unroll=DMA_UNROLL)

    for _ in range(2):
        pltpu.make_async_copy(m2_ref, xs_hbm.at[pl.ds(0, TILE)], sem).wait()


def _dispatch(dest_tiles, m2, xs_init):
    n = m2.shape[0]
    return pl.pallas_call(
        _dispatch_kernel,
        out_shape=jax.ShapeDtypeStruct(xs_init.shape, F32),
        grid=(n // TILE,),
        in_specs=[
            pl.BlockSpec((1, 1, 2 * TILE), lambda i: (i, 0, 0)),
            pl.BlockSpec((TILE, D_MODEL), lambda i: (i, 0)),
            pl.BlockSpec(memory_space=pl.ANY),
        ],
        out_specs=pl.BlockSpec(memory_space=pl.ANY),
        scratch_shapes=[pltpu.SMEM((2 * TILE,), jnp.int32), pltpu.SemaphoreType.DMA, pltpu.SemaphoreType.DMA],
        input_output_aliases={2: 0},
        compiler_params=_cparams(("arbitrary",)),
        name="moe_dispatch",
    )(dest_tiles, m2, xs_init)


def _ffn_kernel(be_ref, bs_ref, nv_ref, x_ref, wg_ref, wu_ref, wd_ref, y_ref):
    del be_ref, bs_ref
    valid = pl.program_id(0) < nv_ref[0]

    @pl.when(valid)
    def _():
        x = x_ref[...].astype(BF16)
        hg = jnp.dot(x, wg_ref[0], preferred_element_type=F32)
        hu = jnp.dot(x, wu_ref[0], preferred_element_type=F32)
        hid = (_silu(hg) * hu).astype(BF16)
        y_ref[...] = jnp.dot(hid, wd_ref[0], preferred_element_type=F32)

    @pl.when(jnp.logical_not(valid))
    def _():
        y_ref[...] = jnp.zeros(y_ref.shape, F32)


def _expert_ffn(blk_e, blk_src, nvalid, xs, wg, wu, wd):
    p = xs.shape[0]
    nb = p // MOE_BM
    grid_spec = pltpu.PrefetchScalarGridSpec(
        num_scalar_prefetch=3,
        grid=(nb,),
        in_specs=[
            pl.BlockSpec((MOE_BM, D_MODEL), lambda i, be, bs, nv: (bs[i], 0)),
            pl.BlockSpec((1, D_MODEL, EXPERT_DFF), lambda i, be, bs, nv: (be[i], 0, 0)),
            pl.BlockSpec((1, D_MODEL, EXPERT_DFF), lambda i, be, bs, nv: (be[i], 0, 0)),
            pl.BlockSpec((1, EXPERT_DFF, D_MODEL), lambda i, be, bs, nv: (be[i], 0, 0)),
        ],
        out_specs=pl.BlockSpec((MOE_BM, D_MODEL), lambda i, be, bs, nv: (i, 0)),
    )
    return pl.pallas_call(
        _ffn_kernel,
        out_shape=jax.ShapeDtypeStruct((p, D_MODEL), F32),
        grid_spec=grid_spec,
        compiler_params=_cparams(("arbitrary",)),
        name="expert_ffn",
    )(blk_e, blk_src, nvalid, xs, wg, wu, wd)


def _combine_kernel(dest_ref, h_ref, w_ref, mod_ref, fg_ref, y_hbm, o_ref, dsm, ybuf, sem_s, sem, *, final):
    cp = pltpu.make_async_copy(dest_ref.at[0, 0], dsm, sem_s)
    cp.start()
    cp.wait()

    def row_copy(t, k):
        return pltpu.make_async_copy(y_hbm.at[pl.ds(dsm[k * TILE + t], 1)],
                                     ybuf.at[k, pl.ds(t, 1)], sem)

    def issue(t, carry):
        row_copy(t, 0).start()
        row_copy(t, 1).start()
        return carry

    lax.fori_loop(0, TILE, issue, 0, unroll=DMA_UNROLL)

    for k in range(2):
        pltpu.make_async_copy(y_hbm.at[pl.ds(0, TILE)], ybuf.at[k], sem).wait()

    w = w_ref[...]
    moe = w[:, 0:1] * ybuf[0] + w[:, 1:2] * ybuf[1]
    h2 = h_ref[...] + mod_ref[0, 0, 5:6, :] * moe
    if final:
        h2 = h2 * lax.rsqrt(jnp.mean(h2 * h2, axis=-1, keepdims=True) + EPS) * fg_ref[...]
    o_ref[...] = h2


def _combine(dest_tiles, h1, wts_col, modtab, final_g, y, batch, s_all, final):
    nt = s_all // TILE
    skip = 1 if final else 0
    nto = nt - skip

    def src(i):
        return (i // nto) * nt + i % nto + skip

    kern = functools.partial(_combine_kernel, final=final)
    return pl.pallas_call(
        kern,
        out_shape=jax.ShapeDtypeStruct((batch * nto * TILE, D_MODEL), F32),
        grid=(batch * nto,),
        in_specs=[
            pl.BlockSpec((1, 1, 2 * TILE), lambda i: (src(i), 0, 0)),
            pl.BlockSpec((TILE, D_MODEL), lambda i: (src(i), 0)),
            pl.BlockSpec((TILE, 2), lambda i: (src(i), 0)),
            pl.BlockSpec((1, 1, 6, D_MODEL), lambda i: (i // nto, jnp.minimum(i % nto + skip, 1), 0, 0)),
            pl.BlockSpec((1, D_MODEL), lambda i: (0, 0)),
            pl.BlockSpec(memory_space=pl.ANY),
        ],
        out_specs=pl.BlockSpec((TILE, D_MODEL), lambda i: (i, 0)),
        scratch_shapes=[
            pltpu.SMEM((2 * TILE,), jnp.int32),
            pltpu.VMEM((2, TILE, D_MODEL), F32),
            pltpu.SemaphoreType.DMA,
            pltpu.SemaphoreType.DMA,
        ],
        compiler_params=_cparams(("arbitrary",)),
        name="moe_combine",
    )(dest_tiles, h1, wts_col, modtab, final_g, y)


def _dispatch_plan(ids, n):
    flat_e = ids.reshape(-1)
    onehot = (flat_e[:, None] == jnp.arange(N_EXPERTS, dtype=jnp.int32)[None, :]).astype(jnp.int32)
    csum = jnp.cumsum(onehot, axis=0)
    counts = csum[-1]
    padded = (counts + MOE_BM - 1) // MOE_BM * MOE_BM
    pend = jnp.cumsum(padded)
    pstart = pend - padded
    dest = jnp.sum(onehot * (pstart[None, :] + csum - 1), axis=1).astype(jnp.int32)
    nb = -(-(2 * n) // MOE_BM) + N_EXPERTS
    nvalid = (pend[-1] // MOE_BM).astype(jnp.int32)
    blk = jnp.arange(nb, dtype=jnp.int32)
    blk_src = jnp.minimum(blk, nvalid - 1)
    blk_e = jnp.minimum(jnp.sum((blk_src[:, None] * MOE_BM >= pend[None, :]).astype(jnp.int32), axis=1),
                        N_EXPERTS - 1).astype(jnp.int32)
    dest2 = dest.reshape(2, n // TILE, TILE)
    dest_tiles = jnp.transpose(dest2, (1, 0, 2)).reshape(n // TILE, 1, 2 * TILE)
    return dest_tiles, blk_e, blk_src, nvalid.reshape(1), nb


def _reorder_w_in(w):
    rest = w[:, 0:4864]
    gates = w[:, 4864:W_IN_USED]
    pad = jnp.zeros((w.shape[0], W_IN_COLS - W_IN_USED), w.dtype)
    return jnp.concatenate([gates, rest, pad], axis=1).astype(BF16)


def _rope_tables(rows, ctx):
    row = jnp.repeat(jnp.arange(rows), GRID_W).astype(F32)
    col = (jnp.arange(rows * GRID_W) % GRID_W).astype(F32)
    inv = ROPE_THETA ** (-jnp.arange(0, ROPE_AXIS_DIM, 2, dtype=F32) / ROPE_AXIS_DIM)
    ar = row[:, None] * inv
    ac = col[:, None] * inv
    ang = jnp.concatenate([ar, ar, ac, ac], axis=-1)
    cos = jnp.concatenate([jnp.ones((ctx, HEAD_DIM), F32), jnp.cos(ang)], axis=0)
    sin = jnp.concatenate([jnp.zeros((ctx, HEAD_DIM), F32), jnp.sin(ang)], axis=0)
    first = (jnp.arange(HEAD_DIM) % ROPE_AXIS_DIM) < (ROPE_AXIS_DIM // 2)
    sa = jnp.where(first[None, :], -sin, 0.0)
    sb = jnp.where(first[None, :], 0.0, sin)
    rep = ATT_DIM // HEAD_DIM
    return jnp.tile(cos, (1, rep)), jnp.tile(sa, (1, rep)), jnp.tile(sb, (1, rep))


def kernel(x, c, ctx, c_ctx, w_mod, b_mod, norm1_g, norm2_g, w_in, conv_w, lb_param, hgrn_norm_g,
           q_norm_g, k_norm_g, w_a_out, w_b_out, w_c_out, w_o, router_w, router_b, w_gate, w_up,
           w_down, final_g):
    batch, seq, _ = x.shape
    n_ctx = ctx.shape[1]
    depth = w_mod.shape[0]
    assert n_ctx == TILE and seq % TILE == 0 and seq % GRID_W == 0 and batch + 1 <= 8
    s_all = n_ctx + seq
    n = batch * s_all

    h = jnp.concatenate([ctx, x], axis=1).reshape(n, D_MODEL)

    cs = jnp.zeros((8, D_MODEL), F32).at[:batch].set(c).at[batch].set(c_ctx)
    mods = _modulation(cs, w_mod, b_mod)
    lat = mods[:, :batch].reshape(depth, batch, 6, D_MODEL)
    cxt = jnp.broadcast_to(mods[:, batch].reshape(depth, 1, 6, D_MODEL), lat.shape)
    modtabs = jnp.stack([cxt, lat], axis=2)

    p = jax.nn.softmax(lb_param.astype(F32), axis=0)
    cum = jnp.cumsum(p, axis=0)
    lower = cum - cum[0:1]

    cos, sa, sb = _rope_tables(seq // GRID_W, n_ctx)
    seg = jnp.arange(ATT_DIM) // HEAD_DIM
    bd = (seg[:, None] == seg[None, :]).astype(BF16)
    rw_t = router_w.T.astype(F32)
    rb = router_b.reshape(N_EXPERTS, 1).astype(F32)
    fg = final_g.reshape(1, D_MODEL)

    out = None
    for layer in range(depth):
        modtab = modtabs[layer]
        z = _in_proj(h, modtab, norm1_g[layer].reshape(1, D_MODEL), _reorder_w_in(w_in[layer]), s_all, n_ctx)

        lbt = jnp.transpose(lower[layer].reshape(2, HGRN_HEADS, HGRN_K), (1, 0, 2))
        ngt = hgrn_norm_g[layer].reshape(HGRN_HEADS, 1, HGRN_K)
        sf, sbk = _hgrn_states(z, lbt, batch, s_all)
        yb = _hgrn_out(z, sf, sbk, lbt, ngt, batch, s_all)

        qg = jnp.tile(q_norm_g[layer], ATT_HEADS).reshape(1, ATT_DIM)
        kg = jnp.tile(k_norm_g[layer], ATT_HEADS).reshape(1, ATT_DIM)
        q_t, k_hm, v_t = _qk_prep(z, cos, sa, sb, qg, kg, bd, s_all)
        score_bound = (HEAD_DIM * HEAD_DIM ** -0.5 * LOG2E * ATT_BOUND_MARGIN
                       * jnp.max(jnp.abs(q_norm_g[layer])) * jnp.max(jnp.abs(k_norm_g[layer])))
        yc = lax.cond(score_bound <= ATT_SAFE_LOG2,
                      lambda a, b, c_: _attention(a, b, c_, batch, s_all, n_ctx, False),
                      lambda a, b, c_: _attention(a, b, c_, batch, s_all, n_ctx, True),
                      q_t, k_hm, v_t)

        h1, m2, ids, wts = _merge(
            h, z, yb, yc, conv_w[layer], w_a_out[layer].astype(BF16), w_b_out[layer].astype(BF16),
            w_c_out[layer].astype(BF16), w_o[layer].astype(BF16), modtab,
            norm2_g[layer].reshape(1, D_MODEL), rw_t, rb, s_all)

        dest_tiles, blk_e, blk_src, nvalid, nb = _dispatch_plan(ids, n)
        xs = _dispatch(dest_tiles, m2, jnp.zeros((nb * MOE_BM, D_MODEL), F32))
        y = _expert_ffn(blk_e, blk_src, nvalid, xs, w_gate[layer].astype(BF16), w_up[layer].astype(BF16),
                        w_down[layer].astype(BF16))
        wts_col = wts.T
        last = layer == depth - 1
        res = _combine(dest_tiles, h1, wts_col, modtab, fg, y, batch, s_all, last)
        if last:
            out = res.reshape(batch, seq, D_MODEL)
        else:
            h = res
    return out
```
